```python
import math
import jax, jax.numpy as jnp
from jax import lax
import numpy as np

D_MODEL = 1024
BATCH = 8
SEQ = 2048
DEPTH = 4

N_A_LAYERS = DEPTH // 2
N_B_LAYERS = DEPTH - N_A_LAYERS
S5_GROUP = 16
S5_GROUPS = D_MODEL // S5_GROUP
S5_STATE = 64
N_HEADS = 16
HEAD_DIM = D_MODEL // N_HEADS
MOBA_BLOCK = 256
MOBA_TOPK = 3
Q_CHUNK = 32
D_FF = 256 * ((8 * D_MODEL // 3 + 255) // 256)
EPS = 1e-6
NEG = -1e30

kernel_name = 'hybrid_s5_moba_macaron_yoco'


def rmsnorm(x, g):
    xf = x.astype(jnp.float32)
    xf = xf * lax.rsqrt(jnp.mean(xf * xf, axis=-1, keepdims=True) + EPS)
    return (xf * g.astype(jnp.float32)).astype(x.dtype)


def swiglu(x, w_in, w_out):
    g, u = jnp.split(x @ w_in, 2, axis=-1)
    return (jax.nn.silu(g) * u) @ w_out


def alibi_slopes():
    return jnp.exp2(-8.0 * jnp.arange(1, N_HEADS + 1, dtype=jnp.float32) / N_HEADS)


def s5_mixer(u, a_re, a_im, b_re, b_im, c_re, c_im, d_skip, log_step, w_glu):
    f32 = jnp.float32
    bsz, seq_len, _ = u.shape
    uf = u.astype(f32)
    ug = uf.reshape(bsz, seq_len, S5_GROUPS, S5_GROUP)
    dt = jnp.exp(log_step.astype(f32))[:, None]
    ar = jnp.minimum(a_re.astype(f32), -1e-4)
    ai = a_im.astype(f32)
    mag = jnp.exp(ar * dt)
    lr = mag * jnp.cos(ai * dt)
    li = mag * jnp.sin(ai * dt)
    nr = lr - 1.0
    den = ar * ar + ai * ai
    fr = (nr * ar + li * ai) / den
    fi = (li * ar - nr * ai) / den
    br = b_re.astype(f32)
    bi = b_im.astype(f32)
    bbr = fr[..., None] * br - fi[..., None] * bi
    bbi = fr[..., None] * bi + fi[..., None] * br
    bu_re = jnp.einsum('blgj,gpj->blgp', ug, bbr)
    bu_im = jnp.einsum('blgj,gpj->blgp', ug, bbi)
    a_r = jnp.broadcast_to(lr, (1, seq_len, S5_GROUPS, S5_STATE))
    a_i = jnp.broadcast_to(li, (1, seq_len, S5_GROUPS, S5_STATE))

    def combine(e1, e2):
        a1r, a1i, b1r, b1i = e1
        a2r, a2i, b2r, b2i = e2
        return (a2r * a1r - a2i * a1i,
                a2r * a1i + a2i * a1r,
                a2r * b1r - a2i * b1i + b2r,
                a2r * b1i + a2i * b1r + b2i)

    _, _, xr, xi = lax.associative_scan(combine, (a_r, a_i, bu_re, bu_im), axis=1)
    y = (jnp.einsum('blgp,gjp->blgj', xr, c_re.astype(f32))
         - jnp.einsum('blgp,gjp->blgj', xi, c_im.astype(f32)))
    y = y.reshape(bsz, seq_len, D_MODEL) + d_skip.astype(f32) * uf
    z = jax.nn.gelu(y).astype(u.dtype)
    za, zb = jnp.split(z @ w_glu, 2, axis=-1)
    return za * jax.nn.sigmoid(zb)


def shared_kv(h, kv_norm, w_kv):
    bsz, seq_len, _ = h.shape
    lp = -(-seq_len // MOBA_BLOCK) * MOBA_BLOCK
    nb = lp // MOBA_BLOCK
    kv = rmsnorm(h, kv_norm) @ w_kv
    kv = jnp.pad(kv, ((0, 0), (0, lp - seq_len), (0, 0)))
    k, v = jnp.split(kv, 2, axis=-1)

    def to_blocks(a):
        return a.reshape(bsz, nb, MOBA_BLOCK, N_HEADS, HEAD_DIM).transpose(0, 3, 1, 2, 4)

    kb = to_blocks(k)
    vb = to_blocks(v)
    kmean = jnp.mean(kb.astype(jnp.float32), axis=3).astype(kb.dtype)
    return kb, vb, kmean


def moba_attention(xn, w_q, w_o, kb, vb, kmean):
    bsz, seq_len, _ = xn.shape
    nb = kb.shape[2]
    lp = nb * MOBA_BLOCK
    n_chunks = lp // Q_CHUNK
    k_sel = min(MOBA_TOPK, nb)
    q = jnp.pad(xn @ w_q, ((0, 0), (0, lp - seq_len), (0, 0)))
    q = q.reshape(bsz, n_chunks, Q_CHUNK, N_HEADS, HEAD_DIM).transpose(1, 0, 2, 3, 4)
    starts = jnp.arange(n_chunks, dtype=jnp.int32) * Q_CHUNK
    slopes = alibi_slopes()
    scale = HEAD_DIM ** -0.5
    b_ix = jnp.arange(bsz)[:, None, None, None]
    h_ix = jnp.arange(N_HEADS)[None, None, :, None]
    offs = jnp.arange(MOBA_BLOCK, dtype=jnp.int32)

    def attend(args):
        qc, start = args
        t = start + jnp.arange(Q_CHUNK, dtype=jnp.int32)
        own = start // MOBA_BLOCK
        gate = jnp.einsum('bqhd,bhnd->bqhn', qc, kmean).astype(jnp.float32)
        gate = jnp.where(jnp.arange(nb) < own, gate, NEG)
        _, idx = lax.top_k(gate, k_sel)
        valid = jnp.arange(k_sel) < own
        kg = kb[b_ix, h_ix, idx]
        vg = vb[b_ix, h_ix, idx]
        s_sel = jnp.einsum('bqhd,bqhrkd->bqhrk', qc, kg).astype(jnp.float32) * scale
        kpos = idx[..., None] * MOBA_BLOCK + offs
        dist = (t[None, :, None, None, None] - kpos).astype(jnp.float32)
        s_sel = s_sel - slopes[None, None, :, None, None] * dist
        s_sel = jnp.where(valid[:, None], s_sel, NEG)
        k_own = lax.dynamic_index_in_dim(kb, own, axis=2, keepdims=False)
        v_own = lax.dynamic_index_in_dim(vb, own, axis=2, keepdims=False)
        s_own = jnp.einsum('bqhd,bhkd->bqhk', qc, k_own).astype(jnp.float32) * scale
        dist_own = (t[:, None] - (own * MOBA_BLOCK + offs)[None, :]).astype(jnp.float32)
        s_own = s_own - slopes[None, None, :, None] * dist_own[None, :, None, :]
        s_own = jnp.where(dist_own[None, :, None, :] >= 0, s_own, NEG)
        logits = jnp.concatenate(
            [s_sel.reshape(bsz, Q_CHUNK, N_HEADS, k_sel * MOBA_BLOCK), s_own], axis=-1)
        p = jax.nn.softmax(logits, axis=-1).astype(vb.dtype)
        p_sel = p[..., :k_sel * MOBA_BLOCK].reshape(bsz, Q_CHUNK, N_HEADS, k_sel, MOBA_BLOCK)
        p_own = p[..., k_sel * MOBA_BLOCK:]
        return (jnp.einsum('bqhrk,bqhrkd->bqhd', p_sel, vg)
                + jnp.einsum('bqhk,bhkd->bqhd', p_own, v_own))

    o = lax.map(attend, (q, starts))
    o = o.transpose(1, 0, 2, 3, 4).reshape(bsz, lp, D_MODEL)[:, :seq_len]
    return o @ w_o


def setup_inputs(seed: int = 0) -> dict:
    key = jax.random.key(seed)
    ks = jax.random.split(key, 22)
    f32 = jnp.float32
    n_a, n_b = N_A_LAYERS, N_B_LAYERS
    G, P, GC = S5_GROUPS, S5_STATE, S5_GROUP

    def nrm(k, shape, scale):
        return jax.random.normal(k, shape, f32) * scale

    def gain(k, shape):
        return 1.0 + 0.02 * jax.random.normal(k, shape, f32)

    n_idx = jnp.arange(P, dtype=f32)
    return {
        'x': jax.random.normal(ks[0], (BATCH, SEQ, D_MODEL), f32),
        'ffn1_norm': gain(ks[1], (DEPTH, D_MODEL)),
        'ffn1_w_in': nrm(ks[2], (DEPTH, D_MODEL, 2 * D_FF), D_MODEL ** -0.5),
        'ffn1_w_out': nrm(ks[3], (DEPTH, D_FF, D_MODEL), D_FF ** -0.5),
        'mix_norm': gain(ks[4], (DEPTH, D_MODEL)),
        'ffn2_norm': gain(ks[5], (DEPTH, D_MODEL)),
        'ffn2_w_in': nrm(ks[6], (DEPTH, D_MODEL, 2 * D_FF), D_MODEL ** -0.5),
        'ffn2_w_out': nrm(ks[7], (DEPTH, D_FF, D_MODEL), D_FF ** -0.5),
        's5_a_re': -0.5 + 0.01 * jax.random.normal(ks[8], (n_a, G, P), f32),
        's5_a_im': math.pi * n_idx + 0.01 * jax.random.normal(ks[9], (n_a, G, P), f32),
        's5_b_re': nrm(ks[10], (n_a, G, P, GC), (2 * GC) ** -0.5),
        's5_b_im': nrm(ks[11], (n_a, G, P, GC), (2 * GC) ** -0.5),
        's5_c_re': nrm(ks[12], (n_a, G, GC, P), P ** -0.5),
        's5_c_im': nrm(ks[13], (n_a, G, GC, P), P ** -0.5),
        's5_d': nrm(ks[14], (n_a, D_MODEL), 1.0),
        's5_log_step': jax.random.uniform(ks[15], (n_a, G), f32, math.log(1e-3), math.log(1e-1)),
        's5_w_glu': nrm(ks[16], (n_a, D_MODEL, 2 * D_MODEL), D_MODEL ** -0.5),
        'kv_norm': gain(ks[17], (D_MODEL,)),
        'w_kv': nrm(ks[18], (D_MODEL, 2 * D_MODEL), D_MODEL ** -0.5),
        'w_q': nrm(ks[19], (n_b, D_MODEL, D_MODEL), D_MODEL ** -0.5),
        'w_o': nrm(ks[20], (n_b, D_MODEL, D_MODEL), D_MODEL ** -0.5),
        'final_norm': gain(ks[21], (D_MODEL,)),
    }


def reference(x, ffn1_norm, ffn1_w_in, ffn1_w_out, mix_norm, ffn2_norm, ffn2_w_in, ffn2_w_out,
              s5_a_re, s5_a_im, s5_b_re, s5_b_im, s5_c_re, s5_c_im, s5_d, s5_log_step, s5_w_glu,
              kv_norm, w_kv, w_q, w_o, final_norm):
    h = x
    kb = vb = kmean = None
    for layer in range(DEPTH):
        if layer == N_A_LAYERS:
            kb, vb, kmean = shared_kv(h, kv_norm, w_kv)
        h = h + 0.5 * swiglu(rmsnorm(h, ffn1_norm[layer]), ffn1_w_in[layer], ffn1_w_out[layer])
        hn = rmsnorm(h, mix_norm[layer])
        if layer < N_A_LAYERS:
            h = h + s5_mixer(hn, s5_a_re[layer], s5_a_im[layer], s5_b_re[layer], s5_b_im[layer],
                             s5_c_re[layer], s5_c_im[layer], s5_d[layer], s5_log_step[layer],
                             s5_w_glu[layer])
        else:
            j = layer - N_A_LAYERS
            h = h + moba_attention(hn, w_q[j], w_o[j], kb, vb, kmean)
        h = h + 0.5 * swiglu(rmsnorm(h, ffn2_norm[layer]), ffn2_w_in[layer], ffn2_w_out[layer])
    return rmsnorm(h, final_norm)
```

```python
import functools
import math

import jax
import jax.numpy as jnp
from jax import lax
from jax.experimental import pallas as pl
from jax.experimental.pallas import tpu as pltpu

S5_GROUP = 16
S5_STATE = 64
N_HEADS = 16
MOBA_BLOCK = 256
MOBA_TOPK = 3
EPS = 1e-6
NEG = -1e30

LANES = 128
S5_CHUNK = LANES
S5_CHUNK_STATES = S5_CHUNK // S5_GROUP * S5_STATE
VMEM_LIMIT = 56 * 1024 * 1024

BF16 = jnp.bfloat16
F32 = jnp.float32


def _rms(x, g):
    return x * lax.rsqrt(jnp.mean(x * x, axis=-1, keepdims=True) + EPS) * g


def _const_spec(shape):
    nd = len(shape)
    return pl.BlockSpec(shape, lambda *_: (0,) * nd, pipeline_mode=pl.Buffered(1))


def _params(semantics):
    return pltpu.CompilerParams(dimension_semantics=semantics, vmem_limit_bytes=VMEM_LIMIT)


def _ffn_kernel(x_ref, g_ref, win_ref, wout_ref, fg_ref, o_ref, *, d_ff, ck, final_norm):
    x = x_ref[...]
    xn = _rms(x, g_ref[...]).astype(BF16)
    acc = jnp.zeros(x.shape, F32)
    for c in range(d_ff // ck):
        gate = jnp.dot(xn, win_ref[:, c * ck:(c + 1) * ck], preferred_element_type=F32)
        up = jnp.dot(xn, win_ref[:, d_ff + c * ck:d_ff + (c + 1) * ck], preferred_element_type=F32)
        a = (gate * jax.nn.sigmoid(gate) * up).astype(BF16)
        acc = acc + jnp.dot(a, wout_ref[c * ck:(c + 1) * ck, :], preferred_element_type=F32)
    y = x + 0.5 * acc
    if final_norm:
        y = _rms(y, fg_ref[...])
    o_ref[...] = y


def _ffn(h, g, w_in, w_out, final_g=None, *, tm=512, ck=256):
    t, d = h.shape
    d_ff = w_out.shape[0]
    final_norm = final_g is not None
    fg = (final_g if final_norm else g).reshape(1, d)
    return pl.pallas_call(
        functools.partial(_ffn_kernel, d_ff=d_ff, ck=ck, final_norm=final_norm),
        out_shape=jax.ShapeDtypeStruct((t, d), F32),
        grid=(t // tm,),
        in_specs=[
            pl.BlockSpec((tm, d), lambda i: (i, 0)),
            _const_spec((1, d)),
            _const_spec((d, 2 * d_ff)),
            _const_spec((d_ff, d)),
            _const_spec((1, d)),
        ],
        out_specs=pl.BlockSpec((tm, d), lambda i: (i, 0)),
        compiler_params=_params(("arbitrary",)),
        name="ffn",
    )(h, g.reshape(1, d), w_in.astype(BF16), w_out.astype(BF16), fg)


def _s5_kernel(h_ref, g_ref, lam_ref, bbd_ref, cbd_ref, dskip_ref, wglu_ref, o_ref,
               state_sc, bu_sc, xs_sc, z_sc, *, tl, n_chunks, batch):
    @pl.when(pl.program_id(0) == 0)
    def _():
        state_sc[...] = jnp.zeros(state_sc.shape, F32)

    ns = S5_CHUNK_STATES
    h = h_ref[...]
    hn = _rms(h, g_ref[...])
    hn_b = hn.astype(BF16)
    for j in range(n_chunks):
        cols = slice(j * S5_CHUNK, (j + 1) * S5_CHUNK)
        bu_sc[...] = jnp.dot(hn_b[:, cols], bbd_ref[j], preferred_element_type=F32)
        lam = lam_ref[j]
        lr, li = lam[:, :ns], lam[:, ns:]

        def step(t, x):
            rows = pl.ds(pl.multiple_of(t * batch, batch), batch)
            bu = bu_sc[rows, :]
            xr, xi = x[:, :ns], x[:, ns:]
            nr = lr * xr - li * xi + bu[:, :ns]
            ni = lr * xi + li * xr + bu[:, ns:]
            x = jnp.concatenate([nr, ni], axis=-1)
            xs_sc[rows, :] = x
            return x

        state_sc[j] = lax.fori_loop(0, tl, step, state_sc[j], unroll=8)
        y = jnp.dot(xs_sc[...].astype(BF16), cbd_ref[j], preferred_element_type=F32)
        y = y + dskip_ref[:, cols] * hn[:, cols]
        z_sc[:, cols] = jax.nn.gelu(y).astype(BF16)
    zz = jnp.dot(z_sc[...], wglu_ref[...], preferred_element_type=F32)
    d = h.shape[-1]
    o_ref[...] = h + zz[:, :d] * jax.nn.sigmoid(zz[:, d:])


def _s5_discretize(a_re, a_im, b_re, b_im, c_re, c_im, log_step, batch):
    g, p = a_re.shape
    gc = S5_CHUNK // S5_GROUP
    n_chunks = g // gc
    dt = jnp.exp(log_step)[:, None]
    ar = jnp.minimum(a_re, -1e-4)
    ai = a_im
    mag = jnp.exp(ar * dt)
    lr = mag * jnp.cos(ai * dt)
    li = mag * jnp.sin(ai * dt)
    nr = lr - 1.0
    den = ar * ar + ai * ai
    fr = (nr * ar + li * ai) / den
    fi = (li * ar - nr * ai) / den
    bbr = fr[..., None] * b_re - fi[..., None] * b_im
    bbi = fr[..., None] * b_im + fi[..., None] * b_re
    eye = jnp.eye(gc, dtype=F32)

    def block_diag_in(bb):
        bb = bb.reshape(n_chunks, gc, p, S5_GROUP)
        return jnp.einsum('cgpk,gh->cgkhp', bb, eye).reshape(n_chunks, gc * S5_GROUP, gc * p)

    def block_diag_out(cc):
        cc = cc.reshape(n_chunks, gc, S5_GROUP, p)
        return jnp.einsum('cgjp,gh->cgphj', cc, eye).reshape(n_chunks, gc * p, gc * S5_GROUP)

    bbd = jnp.concatenate([block_diag_in(bbr), block_diag_in(bbi)], axis=-1).astype(BF16)
    cbd = jnp.concatenate([block_diag_out(c_re), block_diag_out(-c_im)], axis=1).astype(BF16)
    lam = jnp.concatenate([lr.reshape(n_chunks, gc * p), li.reshape(n_chunks, gc * p)], axis=-1)
    lam = jnp.broadcast_to(lam[:, None, :], (n_chunks, batch, 2 * gc * p))
    return lam, bbd, cbd


def _s5_mixer(h, g, a_re, a_im, b_re, b_im, c_re, c_im, d_skip, log_step, w_glu, *, batch, tl=64):
    t, d = h.shape
    n_chunks = d // S5_CHUNK
    ns2 = 2 * S5_CHUNK_STATES
    lam, bbd, cbd = _s5_discretize(a_re, a_im, b_re, b_im, c_re, c_im, log_step, batch)
    rows = tl * batch
    return pl.pallas_call(
        functools.partial(_s5_kernel, tl=tl, n_chunks=n_chunks, batch=batch),
        out_shape=jax.ShapeDtypeStruct((t, d), F32),
        grid=(t // rows,),
        in_specs=[
            pl.BlockSpec((rows, d), lambda i: (i, 0)),
            _const_spec((1, d)),
            _const_spec((n_chunks, batch, ns2)),
            _const_spec((n_chunks, S5_CHUNK, ns2)),
            _const_spec((n_chunks, ns2, S5_CHUNK)),
            _const_spec((1, d)),
            _const_spec((d, 2 * d)),
        ],
        out_specs=pl.BlockSpec((rows, d), lambda i: (i, 0)),
        scratch_shapes=[
            pltpu.VMEM((n_chunks, batch, ns2), F32),
            pltpu.VMEM((rows, ns2), F32),
            pltpu.VMEM((rows, ns2), F32),
            pltpu.VMEM((rows, d), BF16),
        ],
        compiler_params=_params(("arbitrary",)),
        name="s5_mixer",
    )(h, g.reshape(1, d), lam, bbd, cbd, d_skip.reshape(1, d), w_glu.astype(BF16))


def _kv_kernel(x_ref, g_ref, w_ref, kv_ref, mean_ref, *, batch):
    xn = _rms(x_ref[...], g_ref[...]).astype(BF16)
    kv = jnp.dot(xn, w_ref[...], preferred_element_type=F32)
    kv_ref[...] = kv.astype(BF16)
    rows, n = kv.shape
    mean_ref[0] = jnp.sum(kv.reshape(rows // batch, batch, n), axis=0) * (1.0 / (rows // batch))


def _kv_proj(h, g, w_kv, *, batch, tn=512):
    t, d = h.shape
    n = w_kv.shape[1]
    rows = MOBA_BLOCK * batch
    nb = t // rows
    return pl.pallas_call(
        functools.partial(_kv_kernel, batch=batch),
        out_shape=(jax.ShapeDtypeStruct((t, n), BF16), jax.ShapeDtypeStruct((nb, batch, n), F32)),
        grid=(nb, n // tn),
        in_specs=[
            pl.BlockSpec((rows, d), lambda i, j: (i, 0)),
            _const_spec((1, d)),
            pl.BlockSpec((d, tn), lambda i, j: (0, j)),
        ],
        out_specs=(pl.BlockSpec((rows, tn), lambda i, j: (i, j)),
                   pl.BlockSpec((1, batch, tn), lambda i, j: (i, 0, j))),
        compiler_params=_params(("arbitrary", "arbitrary")),
        name="kv_proj",
    )(h, g.reshape(1, d), w_kv.astype(BF16))


def _q_kernel(x_ref, g_ref, w_ref, q_ref):
    xn = _rms(x_ref[...], g_ref[...]).astype(BF16)
    q_ref[...] = jnp.dot(xn, w_ref[...], preferred_element_type=F32)


def _q_proj(h, g, w_q, *, tm=512):
    t, d = h.shape
    n = w_q.shape[1]
    return pl.pallas_call(
        _q_kernel,
        out_shape=jax.ShapeDtypeStruct((t, n), F32),
        grid=(t // tm,),
        in_specs=[pl.BlockSpec((tm, d), lambda i: (i, 0)), _const_spec((1, d)), _const_spec((d, n))],
        out_specs=pl.BlockSpec((tm, n), lambda i: (i, 0)),
        compiler_params=_params(("arbitrary",)),
        name="q_proj",
    )(h, g.reshape(1, d), w_q.astype(BF16))


def _oproj_kernel(o_ref, w_ref, h_ref, out_ref):
    out_ref[...] = h_ref[...] + jnp.dot(o_ref[...], w_ref[...], preferred_element_type=F32)


def _o_proj(o, w_o, h, *, tm=512):
    t, d = h.shape
    return pl.pallas_call(
        _oproj_kernel,
        out_shape=jax.ShapeDtypeStruct((t, d), F32),
        grid=(t // tm,),
        in_specs=[pl.BlockSpec((tm, d), lambda i: (i, 0)), _const_spec((d, d)),
                  pl.BlockSpec((tm, d), lambda i: (i, 0))],
        out_specs=pl.BlockSpec((tm, d), lambda i: (i, 0)),
        compiler_params=_params(("arbitrary",)),
        name="o_proj",
    )(o, w_o.astype(BF16), h)


def _moba_kernel(slopes_ref, q_ref, k_ref, v_ref, km_ref, o_ref, *, nb, head_dim, pairs):
    bs = MOBA_BLOCK
    pair = pl.program_id(0) % pairs
    i = pl.program_id(1)
    q = q_ref[...]
    km = km_ref[...]
    lane = lax.broadcasted_iota(jnp.int32, (1, LANES), 1)
    rel = (lax.broadcasted_iota(jnp.int32, (bs, bs), 0)
           - lax.broadcasted_iota(jnp.int32, (bs, bs), 1)).astype(F32)
    blk = lax.broadcasted_iota(jnp.int32, (bs, nb), 1)
    nt = (((1,), (1,)), ((), ()))
    scale = head_dim ** -0.5
    heads = LANES // head_dim

    k_own = k_ref[pl.ds(pl.multiple_of(i * bs, bs), bs), :]
    v_own = v_ref[pl.ds(pl.multiple_of(i * bs, bs), bs), :]

    qs, alibi, selbias, carry = [], [], [], []
    for hh in range(heads):
        slope = slopes_ref[pair * heads + hh]
        qh = jnp.where(lane // head_dim == hh, q, 0.0)
        gate = lax.dot_general(qh, km, nt, precision=lax.Precision.HIGHEST, preferred_element_type=F32)
        valid = blk < i
        gate = jnp.where(valid, gate, NEG)
        rank = jnp.zeros((bs, nb), jnp.int32)
        for m in range(nb):
            gm = gate[:, m:m + 1]
            rank = rank + ((gm > gate) | ((gm == gate) & (m < blk))).astype(jnp.int32)
        sel = (rank < MOBA_TOPK) & valid
        dist0 = ((i - blk) * bs).astype(F32)
        selbias.append(jnp.where(sel, -slope * dist0, NEG))
        a = -slope * rel
        alibi.append(a)
        qb = (qh * scale).astype(BF16)
        qs.append(qb)
        s = lax.dot_general(qb, k_own, nt, preferred_element_type=F32) + jnp.where(rel >= 0, a, NEG)
        m0 = jnp.max(s, axis=-1, keepdims=True)
        p = jnp.exp(s - m0)
        l0 = jnp.sum(p, axis=-1, keepdims=True)
        acc0 = jnp.dot(p.astype(BF16), v_own, preferred_element_type=F32)
        carry += [m0, l0, acc0]

    def body(n, carry):
        rows = pl.ds(pl.multiple_of(n * bs, bs), bs)
        k_n = k_ref[rows, :]
        v_n = v_ref[rows, :]
        out = []
        for hh in range(heads):
            m_prev, l_prev, acc_prev = carry[3 * hh:3 * hh + 3]
            rowbias = jnp.sum(jnp.where(blk == n, selbias[hh], 0.0), axis=-1, keepdims=True)
            s = lax.dot_general(qs[hh], k_n, nt, preferred_element_type=F32) + alibi[hh] + rowbias
            m_new = jnp.maximum(m_prev, jnp.max(s, axis=-1, keepdims=True))
            alpha = jnp.exp(m_prev - m_new)
            p = jnp.exp(s - m_new)
            l_new = alpha * l_prev + jnp.sum(p, axis=-1, keepdims=True)
            acc_new = alpha * acc_prev + jnp.dot(p.astype(BF16), v_n, preferred_element_type=F32)
            out += [m_new, l_new, acc_new]
        return tuple(out)

    carry = lax.fori_loop(0, i, body, tuple(carry))
    o = jnp.zeros((bs, LANES), F32)
    for hh in range(heads):
        o = jnp.where(lane // head_dim == hh, carry[3 * hh + 2] / carry[3 * hh + 1], o)
    o_ref[...] = o.astype(BF16)


def _moba(q, kv, kmean, *, batch, seq, d):
    nb = seq // MOBA_BLOCK
    pairs = d // LANES
    head_dim = d // N_HEADS
    slopes = jnp.exp2(-8.0 * jnp.arange(1, N_HEADS + 1, dtype=F32) / N_HEADS)
    q2 = q.reshape(seq, batch * d)
    kv2 = kv.reshape(seq, batch * 2 * d)
    km2 = kmean.reshape(nb, batch * 2 * d)
    grid_spec = pltpu.PrefetchScalarGridSpec(
        num_scalar_prefetch=1,
        grid=(batch * pairs, nb),
        in_specs=[
            pl.BlockSpec((MOBA_BLOCK, LANES), lambda bp, i, s: (i, bp)),
            pl.BlockSpec((seq, LANES), lambda bp, i, s: (0, bp // pairs * 2 * pairs + bp % pairs)),
            pl.BlockSpec((seq, LANES), lambda bp, i, s: (0, bp // pairs * 2 * pairs + pairs + bp % pairs)),
            pl.BlockSpec((nb, LANES), lambda bp, i, s: (0, bp // pairs * 2 * pairs + bp % pairs)),
        ],
        out_specs=pl.BlockSpec((MOBA_BLOCK, LANES), lambda bp, i, s: (i, bp)),
    )
    o = pl.pallas_call(
        functools.partial(_moba_kernel, nb=nb, head_dim=head_dim, pairs=pairs),
        out_shape=jax.ShapeDtypeStruct((seq, batch * d), BF16),
        grid_spec=grid_spec,
        compiler_params=_params(("arbitrary", "arbitrary")),
        name="moba",
    )(slopes, q2, kv2, kv2, km2)
    return o.reshape(seq * batch, d)


def kernel(x, ffn1_norm, ffn1_w_in, ffn1_w_out, mix_norm, ffn2_norm, ffn2_w_in, ffn2_w_out,
           s5_a_re, s5_a_im, s5_b_re, s5_b_im, s5_c_re, s5_c_im, s5_d, s5_log_step, s5_w_glu,
           kv_norm, w_kv, w_q, w_o, final_norm):
    batch, seq, d = x.shape
    depth = ffn1_norm.shape[0]
    n_a = s5_a_re.shape[0]
    h = jnp.transpose(x, (1, 0, 2)).reshape(seq * batch, d)
    kv = kmean = None
    for layer in range(depth):
        if layer == n_a:
            kv, kmean = _kv_proj(h, kv_norm, w_kv, batch=batch)
        h = _ffn(h, ffn1_norm[layer], ffn1_w_in[layer], ffn1_w_out[layer])
        if layer < n_a:
            h = _s5_mixer(h, mix_norm[layer], s5_a_re[layer], s5_a_im[layer], s5_b_re[layer],
                          s5_b_im[layer], s5_c_re[layer], s5_c_im[layer], s5_d[layer],
                          s5_log_step[layer], s5_w_glu[layer], batch=batch)
        else:
            j = layer - n_a
            q = _q_proj(h, mix_norm[layer], w_q[j])
            o = _moba(q, kv, kmean, batch=batch, seq=seq, d=d)
            h = _o_proj(o, w_o[j], h)
        last = layer == depth - 1
        h = _ffn(h, ffn2_norm[layer], ffn2_w_in[layer], ffn2_w_out[layer],
                 final_g=final_norm if last else None)
    return jnp.transpose(h.reshape(seq, batch, d), (1, 0, 2))
```

```python
import functools

import jax
import jax.numpy as jnp
from jax import lax
from jax.experimental import pallas as pl
from jax.experimental.pallas import tpu as pltpu

S5_GROUP = 16
S5_STATE = 64
N_HEADS = 16
MOBA_BLOCK = 256
MOBA_TOPK = 3
EPS = 1e-6
NEG = -1e30

LANES = 128
S5_CHUNK = LANES
S5_CHUNK_STATES = S5_CHUNK // S5_GROUP * S5_STATE
VMEM_LIMIT = 56 * 1024 * 1024

BF16 = jnp.bfloat16
F32 = jnp.float32
NT = (((1,), (1,)), ((), ()))


def _rms(x, g):
    return x * lax.rsqrt(jnp.mean(x * x, axis=-1, keepdims=True) + EPS) * g


def _const_spec(shape):
    nd = len(shape)
    return pl.BlockSpec(shape, lambda *_: (0,) * nd, pipeline_mode=pl.Buffered(1))


def _params(semantics):
    return pltpu.CompilerParams(dimension_semantics=semantics, vmem_limit_bytes=VMEM_LIMIT)


def _ffn_kernel(x_ref, g_ref, win_ref, wout_ref, fg_ref, o_ref, *, d_ff, ck, final_norm):
    x = x_ref[...]
    xn = _rms(x, g_ref[...]).astype(BF16)
    acc = jnp.zeros(x.shape, F32)
    for c in range(d_ff // ck):
        gate = jnp.dot(xn, win_ref[:, c * ck:(c + 1) * ck], preferred_element_type=F32)
        up = jnp.dot(xn, win_ref[:, d_ff + c * ck:d_ff + (c + 1) * ck], preferred_element_type=F32)
        a = (gate * jax.nn.sigmoid(gate) * up).astype(BF16)
        acc = acc + jnp.dot(a, wout_ref[c * ck:(c + 1) * ck, :], preferred_element_type=F32)
    y = x + 0.5 * acc
    if final_norm:
        y = _rms(y, fg_ref[...])
    o_ref[...] = y


def _ffn(h, g, w_in, w_out, final_g=None, *, tm=512, ck=256):
    t, d = h.shape
    d_ff = w_out.shape[0]
    final_norm = final_g is not None
    fg = (final_g if final_norm else g).reshape(1, d)
    return pl.pallas_call(
        functools.partial(_ffn_kernel, d_ff=d_ff, ck=ck, final_norm=final_norm),
        out_shape=jax.ShapeDtypeStruct((t, d), F32),
        grid=(t // tm,),
        in_specs=[
            pl.BlockSpec((tm, d), lambda i: (i, 0)),
            _const_spec((1, d)),
            _const_spec((d, 2 * d_ff)),
            _const_spec((d_ff, d)),
            _const_spec((1, d)),
        ],
        out_specs=pl.BlockSpec((tm, d), lambda i: (i, 0)),
        compiler_params=_params(("arbitrary",)),
        name="ffn",
    )(h, g.reshape(1, d), w_in.astype(BF16), w_out.astype(BF16), fg)


def _to_time_major(x, slab_sc, tl, batch):
    slabs = x.shape[1] // LANES
    for s in range(slabs):
        for b in range(batch):
            slab_sc[s, pl.ds(b, tl, stride=batch), :] = x[b * tl:(b + 1) * tl, s * LANES:(s + 1) * LANES]
    return jnp.concatenate([slab_sc[s] for s in range(slabs)], axis=-1)


def _to_batch_major(y, slab_sc, tl, batch):
    slabs = y.shape[1] // LANES
    for s in range(slabs):
        slab_sc[s] = y[:, s * LANES:(s + 1) * LANES]
    return jnp.concatenate(
        [jnp.concatenate([slab_sc[s, pl.ds(b, tl, stride=batch), :] for s in range(slabs)], axis=-1)
         for b in range(batch)], axis=0)


def _s5_kernel(h_ref, g_ref, lam_ref, bbd_ref, cbd_ref, dskip_ref, wglu_ref, o_ref,
               state_sc, bu_sc, xs_sc, z_sc, slab_sc, *, tl, n_chunks, batch):
    @pl.when(pl.program_id(0) == 0)
    def _():
        state_sc[...] = jnp.zeros(state_sc.shape, F32)

    ns = S5_CHUNK_STATES
    d = h_ref.shape[-1]
    h = h_ref[...].reshape(batch * tl, d)
    hn = _to_time_major(_rms(h, g_ref[...]), slab_sc, tl, batch)
    hn_b = hn.astype(BF16)
    for j in range(n_chunks):
        cols = slice(j * S5_CHUNK, (j + 1) * S5_CHUNK)
        bu_sc[...] = jnp.dot(hn_b[:, cols], bbd_ref[j], preferred_element_type=F32)
        lam = lam_ref[j]
        lr, li = lam[:, :ns], lam[:, ns:]

        def step(t, x):
            rows = pl.ds(pl.multiple_of(t * batch, batch), batch)
            bu = bu_sc[rows, :]
            xr, xi = x[:, :ns], x[:, ns:]
            nr = lr * xr - li * xi + bu[:, :ns]
            ni = lr * xi + li * xr + bu[:, ns:]
            x = jnp.concatenate([nr, ni], axis=-1)
            xs_sc[rows, :] = x
            return x

        state_sc[j] = lax.fori_loop(0, tl, step, state_sc[j], unroll=8)
        y = jnp.dot(xs_sc[...].astype(BF16), cbd_ref[j], preferred_element_type=F32)
        y = y + dskip_ref[:, cols] * hn[:, cols]
        z_sc[:, cols] = jax.nn.gelu(y).astype(BF16)
    zz = jnp.dot(z_sc[...], wglu_ref[...], preferred_element_type=F32)
    mixed = _to_batch_major(zz[:, :d] * jax.nn.sigmoid(zz[:, d:]), slab_sc, tl, batch)
    o_ref[...] = (h + mixed).reshape(batch, tl, d)


def _s5_discretize(a_re, a_im, b_re, b_im, c_re, c_im, log_step, batch):
    g, p = a_re.shape
    gc = S5_CHUNK // S5_GROUP
    n_chunks = g // gc
    dt = jnp.exp(log_step)[:, None]
    ar = jnp.minimum(a_re, -1e-4)
    ai = a_im
    mag = jnp.exp(ar * dt)
    lr = mag * jnp.cos(ai * dt)
    li = mag * jnp.sin(ai * dt)
    nr = lr - 1.0
    den = ar * ar + ai * ai
    fr = (nr * ar + li * ai) / den
    fi = (li * ar - nr * ai) / den
    bbr = fr[..., None] * b_re - fi[..., None] * b_im
    bbi = fr[..., None] * b_im + fi[..., None] * b_re
    eye = jnp.eye(gc, dtype=F32)

    def block_diag_in(bb):
        bb = bb.reshape(n_chunks, gc, p, S5_GROUP)
        return jnp.einsum('cgpk,gh->cgkhp', bb, eye).reshape(n_chunks, gc * S5_GROUP, gc * p)

    def block_diag_out(cc):
        cc = cc.reshape(n_chunks, gc, S5_GROUP, p)
        return jnp.einsum('cgjp,gh->cgphj', cc, eye).reshape(n_chunks, gc * p, gc * S5_GROUP)

    bbd = jnp.concatenate([block_diag_in(bbr), block_diag_in(bbi)], axis=-1).astype(BF16)
    cbd = jnp.concatenate([block_diag_out(c_re), block_diag_out(-c_im)], axis=1).astype(BF16)
    lam = jnp.concatenate([lr.reshape(n_chunks, gc * p), li.reshape(n_chunks, gc * p)], axis=-1)
    lam = jnp.broadcast_to(lam[:, None, :], (n_chunks, batch, 2 * gc * p))
    return lam, bbd, cbd


def _s5_mixer(h, g, a_re, a_im, b_re, b_im, c_re, c_im, d_skip, log_step, w_glu, *, batch, tl=64):
    t, d = h.shape
    seq = t // batch
    n_chunks = d // S5_CHUNK
    ns2 = 2 * S5_CHUNK_STATES
    lam, bbd, cbd = _s5_discretize(a_re, a_im, b_re, b_im, c_re, c_im, log_step, batch)
    rows = tl * batch
    out = pl.pallas_call(
        functools.partial(_s5_kernel, tl=tl, n_chunks=n_chunks, batch=batch),
        out_shape=jax.ShapeDtypeStruct((batch, seq, d), F32),
        grid=(seq // tl,),
        in_specs=[
            pl.BlockSpec((batch, tl, d), lambda i: (0, i, 0)),
            _const_spec((1, d)),
            _const_spec((n_chunks, batch, ns2)),
            _const_spec((n_chunks, S5_CHUNK, ns2)),
            _const_spec((n_chunks, ns2, S5_CHUNK)),
            _const_spec((1, d)),
            _const_spec((d, 2 * d)),
        ],
        out_specs=pl.BlockSpec((batch, tl, d), lambda i: (0, i, 0)),
        scratch_shapes=[
            pltpu.VMEM((n_chunks, batch, ns2), F32),
            pltpu.VMEM((rows, ns2), F32),
            pltpu.VMEM((rows, ns2), F32),
            pltpu.VMEM((rows, d), BF16),
            pltpu.VMEM((d // LANES, rows, LANES), F32),
        ],
        compiler_params=_params(("arbitrary",)),
        name="s5_mixer",
    )(h.reshape(batch, seq, d), g.reshape(1, d), lam, bbd, cbd, d_skip.reshape(1, d), w_glu.astype(BF16))
    return out.reshape(t, d)


def _kv_kernel(x_ref, g_ref, wk_ref, wvt_ref, k_ref, vt_ref, km_ref, *, ck):
    xn = _rms(x_ref[0], g_ref[...]).astype(BF16)
    seq, d = xn.shape
    nb = seq // MOBA_BLOCK
    for c in range(d // ck):
        cols = slice(c * ck, (c + 1) * ck)
        kc = jnp.dot(xn, wk_ref[:, cols], preferred_element_type=F32)
        k_ref[0, :, cols] = kc.astype(BF16)
        km_ref[0, :, cols] = jnp.mean(kc.reshape(nb, MOBA_BLOCK, ck), axis=1)
        vt_ref[0, cols, :] = lax.dot_general(wvt_ref[cols, :], xn, NT,
                                             preferred_element_type=F32).astype(BF16)


def _kv_proj(h, g, w_kv, *, batch, ck=256):
    t, d = h.shape
    seq = t // batch
    nb = seq // MOBA_BLOCK
    w_k = w_kv[:, :d].astype(BF16)
    w_vt = w_kv[:, d:].T.astype(BF16)
    return pl.pallas_call(
        functools.partial(_kv_kernel, ck=ck),
        out_shape=(jax.ShapeDtypeStruct((batch, seq, d), BF16),
                   jax.ShapeDtypeStruct((batch, d, seq), BF16),
                   jax.ShapeDtypeStruct((batch, nb, d), F32)),
        grid=(batch,),
        in_specs=[
            pl.BlockSpec((1, seq, d), lambda b: (b, 0, 0)),
            _const_spec((1, d)),
            _const_spec((d, d)),
            _const_spec((d, d)),
        ],
        out_specs=(pl.BlockSpec((1, seq, d), lambda b: (b, 0, 0)),
                   pl.BlockSpec((1, d, seq), lambda b: (b, 0, 0)),
                   pl.BlockSpec((1, nb, d), lambda b: (b, 0, 0))),
        compiler_params=_params(("arbitrary",)),
        name="kv_proj",
    )(h.reshape(batch, seq, d), g.reshape(1, d), w_k, w_vt)


def _q_kernel(x_ref, g_ref, w_ref, q_ref):
    xn = _rms(x_ref[...], g_ref[...]).astype(BF16)
    q_ref[...] = jnp.dot(xn, w_ref[...], preferred_element_type=F32)


def _q_proj(h, g, w_q, *, tm=512):
    t, d = h.shape
    n = w_q.shape[1]
    return pl.pallas_call(
        _q_kernel,
        out_shape=jax.ShapeDtypeStruct((t, n), F32),
        grid=(t // tm,),
        in_specs=[pl.BlockSpec((tm, d), lambda i: (i, 0)), _const_spec((1, d)), _const_spec((d, n))],
        out_specs=pl.BlockSpec((tm, n), lambda i: (i, 0)),
        compiler_params=_params(("arbitrary",)),
        name="q_proj",
    )(h, g.reshape(1, d), w_q.astype(BF16))


def _oproj_kernel(o_ref, w_ref, h_ref, out_ref):
    out_ref[...] = h_ref[...] + jnp.dot(o_ref[...], w_ref[...], preferred_element_type=F32)


def _o_proj(o, w_o, h, *, tm=512):
    t, d = h.shape
    return pl.pallas_call(
        _oproj_kernel,
        out_shape=jax.ShapeDtypeStruct((t, d), F32),
        grid=(t // tm,),
        in_specs=[pl.BlockSpec((tm, d), lambda i: (i, 0)), _const_spec((d, d)),
                  pl.BlockSpec((tm, d), lambda i: (i, 0))],
        out_specs=pl.BlockSpec((tm, d), lambda i: (i, 0)),
        compiler_params=_params(("arbitrary",)),
        name="o_proj",
    )(o, w_o.astype(BF16), h)


def _moba_kernel(slopes_ref, q_ref, k_ref, vt_ref, km_ref, o_ref, ot_sc, *, nb, head_dim, pairs):
    bs = MOBA_BLOCK
    pair = pl.program_id(0) % pairs
    heads = LANES // head_dim
    scale = head_dim ** -0.5
    q = q_ref[0]
    km = km_ref[0]
    lane = lax.broadcasted_iota(jnp.int32, (1, LANES), 1)
    rel_t = (lax.broadcasted_iota(jnp.int32, (bs, bs), 1)
             - lax.broadcasted_iota(jnp.int32, (bs, bs), 0)).astype(F32)
    blk = lax.broadcasted_iota(jnp.int32, (nb, bs), 0)

    for hh in range(heads):
        slope = slopes_ref[pair * heads + hh]
        qh = jnp.where(lane // head_dim == hh, q, 0.0)
        gate_t = lax.dot_general(km, qh, NT, precision=lax.Precision.HIGHEST, preferred_element_type=F32)
        qb = (qh * scale).astype(BF16)
        alibi = -slope * rel_t
        alibi_own = jnp.where(rel_t >= 0, alibi, NEG)
        for i in range(nb):
            nk = (i + 1) * bs
            s = lax.dot_general(k_ref[0, 0:nk, :], qb[i * bs:(i + 1) * bs], NT,
                                preferred_element_type=F32)
            if i > MOBA_TOPK:
                g = gate_t[:, i * bs:(i + 1) * bs]
                rank = jnp.zeros((nb, bs), jnp.int32)
                for m in range(i):
                    gm = g[m:m + 1, :]
                    rank = rank + ((gm > g) | ((gm == g) & (m < blk))).astype(jnp.int32)
                sel = rank < MOBA_TOPK
            parts = []
            for n in range(i):
                dist = -slope * float((i - n) * bs)
                if i > MOBA_TOPK:
                    dist = jnp.where(sel[n:n + 1, :], dist, NEG)
                parts.append(s[n * bs:(n + 1) * bs] + alibi + dist)
            parts.append(s[i * bs:] + alibi_own)
            s = jnp.concatenate(parts, axis=0)
            m_row = jnp.max(s, axis=0, keepdims=True)
            p = jnp.exp(s - m_row)
            l_row = jnp.sum(p, axis=0, keepdims=True)
            o_t = jnp.dot(vt_ref[0, hh * head_dim:(hh + 1) * head_dim, 0:nk], p.astype(BF16),
                          preferred_element_type=F32)
            ot_sc[hh * head_dim:(hh + 1) * head_dim, i * bs:(i + 1) * bs] = o_t / l_row
    o_ref[0] = ot_sc[...].T.astype(BF16)


def _moba(q, k, vt, kmean, *, batch, seq, d):
    nb = seq // MOBA_BLOCK
    pairs = d // LANES
    head_dim = d // N_HEADS
    slopes = jnp.exp2(-8.0 * jnp.arange(1, N_HEADS + 1, dtype=F32) / N_HEADS)
    grid_spec = pltpu.PrefetchScalarGridSpec(
        num_scalar_prefetch=1,
        grid=(batch * pairs,),
        in_specs=[
            pl.BlockSpec((1, seq, LANES), lambda bp, s: (bp // pairs, 0, bp % pairs)),
            pl.BlockSpec((1, seq, LANES), lambda bp, s: (bp // pairs, 0, bp % pairs)),
            pl.BlockSpec((1, LANES, seq), lambda bp, s: (bp // pairs, bp % pairs, 0)),
            pl.BlockSpec((1, nb, LANES), lambda bp, s: (bp // pairs, 0, bp % pairs)),
        ],
        out_specs=pl.BlockSpec((1, seq, LANES), lambda bp, s: (bp // pairs, 0, bp % pairs)),
        scratch_shapes=[pltpu.VMEM((LANES, seq), F32)],
    )
    o = pl.pallas_call(
        functools.partial(_moba_kernel, nb=nb, head_dim=head_dim, pairs=pairs),
        out_shape=jax.ShapeDtypeStruct((batch, seq, d), BF16),
        grid_spec=grid_spec,
        compiler_params=_params(("arbitrary",)),
        name="moba",
    )(slopes, q.reshape(batch, seq, d), k, vt, kmean)
    return o.reshape(batch * seq, d)


def kernel(x, ffn1_norm, ffn1_w_in, ffn1_w_out, mix_norm, ffn2_norm, ffn2_w_in, ffn2_w_out,
           s5_a_re, s5_a_im, s5_b_re, s5_b_im, s5_c_re, s5_c_im, s5_d, s5_log_step, s5_w_glu,
           kv_norm, w_kv, w_q, w_o, final_norm):
    batch, seq, d = x.shape
    depth = ffn1_norm.shape[0]
    n_a = s5_a_re.shape[0]
    h = x.reshape(batch * seq, d)
    k = vt = kmean = None
    for layer in range(depth):
        if layer == n_a:
            k, vt, kmean = _kv_proj(h, kv_norm, w_kv, batch=batch)
        h = _ffn(h, ffn1_norm[layer], ffn1_w_in[layer], ffn1_w_out[layer])
        if layer < n_a:
            h = _s5_mixer(h, mix_norm[layer], s5_a_re[layer], s5_a_im[layer], s5_b_re[layer],
                          s5_b_im[layer], s5_c_re[layer], s5_c_im[layer], s5_d[layer],
                          s5_log_step[layer], s5_w_glu[layer], batch=batch)
        else:
            j = layer - n_a
            q = _q_proj(h, mix_norm[layer], w_q[j])
            o = _moba(q, k, vt, kmean, batch=batch, seq=seq, d=d)
            h = _o_proj(o, w_o[j], h)
        last = layer == depth - 1
        h = _ffn(h, ffn2_norm[layer], ffn2_w_in[layer], ffn2_w_out[layer],
                 final_g=final_norm if last else None)
    return h.reshape(batch, seq, d)
```

```python
import functools

import jax
import jax.numpy as jnp
from jax import lax
from jax.experimental import pallas as pl
from jax.experimental.pallas import tpu as pltpu

S5_GROUP = 16
S5_STATE = 64
N_HEADS = 16
MOBA_BLOCK = 256
MOBA_TOPK = 3
EPS = 1e-6
NEG = -1e30

LANES = 128
S5_CHUNK = LANES
S5_CHUNK_STATES = S5_CHUNK // S5_GROUP * S5_STATE
VMEM_LIMIT = 56 * 1024 * 1024

BF16 = jnp.bfloat16
F32 = jnp.float32
NT = (((1,), (1,)), ((), ()))


def _rms(x, g):
    return x * lax.rsqrt(jnp.mean(x * x, axis=-1, keepdims=True) + EPS) * g


def _const_spec(shape, index=None):
    index = (0,) * len(shape) if index is None else index
    return pl.BlockSpec(shape, lambda *_: index, pipeline_mode=pl.Buffered(1))


def _params(semantics):
    return pltpu.CompilerParams(dimension_semantics=semantics, vmem_limit_bytes=VMEM_LIMIT)


def _ffn_kernel(x_ref, g_ref, win_ref, wout_ref, fg_ref, o_ref, *, d_ff, ck, final_norm):
    x = x_ref[...]
    xn = _rms(x, g_ref[...]).astype(BF16)
    acc = jnp.zeros(x.shape, F32)
    for c in range(d_ff // ck):
        gate = jnp.dot(xn, win_ref[:, c * ck:(c + 1) * ck], preferred_element_type=F32)
        up = jnp.dot(xn, win_ref[:, d_ff + c * ck:d_ff + (c + 1) * ck], preferred_element_type=F32)
        a = (gate * jax.nn.sigmoid(gate) * up).astype(BF16)
        acc = acc + jnp.dot(a, wout_ref[c * ck:(c + 1) * ck, :], preferred_element_type=F32)
    y = x + 0.5 * acc
    if final_norm:
        y = _rms(y, fg_ref[...])
    o_ref[...] = y


def _ffn(h, g, w_in, w_out, layer, final_g=None, *, tm=512, ck=256):
    t, d = h.shape
    d_ff = w_out.shape[1]
    final_norm = final_g is not None
    fg = (final_g if final_norm else g).reshape(1, d)
    return pl.pallas_call(
        functools.partial(_ffn_kernel, d_ff=d_ff, ck=ck, final_norm=final_norm),
        out_shape=jax.ShapeDtypeStruct((t, d), F32),
        grid=(t // tm,),
        in_specs=[
            pl.BlockSpec((tm, d), lambda i: (i, 0)),
            _const_spec((1, d)),
            _const_spec((None, d, 2 * d_ff), (layer, 0, 0)),
            _const_spec((None, d_ff, d), (layer, 0, 0)),
            _const_spec((1, d)),
        ],
        out_specs=pl.BlockSpec((tm, d), lambda i: (i, 0)),
        compiler_params=_params(("arbitrary",)),
        name="ffn",
    )(h, g.reshape(1, d), w_in, w_out, fg)


def _to_time_major(x, slab_sc, tl, batch):
    slabs = x.shape[1] // LANES
    for s in range(slabs):
        for b in range(batch):
            slab_sc[s, pl.ds(b, tl, stride=batch), :] = x[b * tl:(b + 1) * tl, s * LANES:(s + 1) * LANES]
    return jnp.concatenate([slab_sc[s] for s in range(slabs)], axis=-1)


def _to_batch_major(y, slab_sc, tl, batch):
    slabs = y.shape[1] // LANES
    for s in range(slabs):
        slab_sc[s] = y[:, s * LANES:(s + 1) * LANES]
    return jnp.concatenate(
        [jnp.concatenate([slab_sc[s, pl.ds(b, tl, stride=batch), :] for s in range(slabs)], axis=-1)
         for b in range(batch)], axis=0)


def _s5_kernel(h_ref, g_ref, lam_ref, bbd_ref, cbd_ref, dskip_ref, wglu_ref, o_ref,
               state_sc, slab_sc, *, tl, n_chunks, batch):
    @pl.when(pl.program_id(0) == 0)
    def _():
        state_sc[...] = jnp.zeros(state_sc.shape, F32)

    ns = S5_CHUNK_STATES
    d = h_ref.shape[-1]
    h = h_ref[...].reshape(batch * tl, d)
    hn = _to_time_major(_rms(h, g_ref[...]), slab_sc, tl, batch)
    hn_b = hn.astype(BF16)
    z = []
    for j in range(n_chunks):
        cols = slice(j * S5_CHUNK, (j + 1) * S5_CHUNK)
        bu = jnp.dot(hn_b[:, cols], bbd_ref[j], preferred_element_type=F32)
        lam = lam_ref[j]
        lr, li = lam[:, :ns], lam[:, ns:]
        x = state_sc[j]
        xs = []
        for t in range(tl):
            bu_t = bu[t * batch:(t + 1) * batch]
            xr, xi = x[:, :ns], x[:, ns:]
            nr = lr * xr - li * xi + bu_t[:, :ns]
            ni = lr * xi + li * xr + bu_t[:, ns:]
            x = jnp.concatenate([nr, ni], axis=-1)
            xs.append(x)
        state_sc[j] = x
        y = jnp.dot(jnp.concatenate(xs, axis=0).astype(BF16), cbd_ref[j], preferred_element_type=F32)
        y = y + dskip_ref[:, cols] * hn[:, cols]
        z.append(jax.nn.gelu(y).astype(BF16))
    zz = jnp.dot(jnp.concatenate(z, axis=-1), wglu_ref[...], preferred_element_type=F32)
    mixed = _to_batch_major(zz[:, :d] * jax.nn.sigmoid(zz[:, d:]), slab_sc, tl, batch)
    o_ref[...] = (h + mixed).reshape(batch, tl, d)


def _s5_discretize(a_re, a_im, b_re, b_im, c_re, c_im, log_step, batch):
    g, p = a_re.shape
    gc = S5_CHUNK // S5_GROUP
    n_chunks = g // gc
    dt = jnp.exp(log_step)[:, None]
    ar = jnp.minimum(a_re, -1e-4)
    ai = a_im
    mag = jnp.exp(ar * dt)
    lr = mag * jnp.cos(ai * dt)
    li = mag * jnp.sin(ai * dt)
    nr = lr - 1.0
    den = ar * ar + ai * ai
    fr = (nr * ar + li * ai) / den
    fi = (li * ar - nr * ai) / den
    bbr = fr[..., None] * b_re - fi[..., None] * b_im
    bbi = fr[..., None] * b_im + fi[..., None] * b_re
    eye = jnp.eye(gc, dtype=F32)

    def block_diag_in(bb):
        bb = bb.reshape(n_chunks, gc, p, S5_GROUP)
        return jnp.einsum('cgpk,gh->cgkhp', bb, eye).reshape(n_chunks, gc * S5_GROUP, gc * p)

    def block_diag_out(cc):
        cc = cc.reshape(n_chunks, gc, S5_GROUP, p)
        return jnp.einsum('cgjp,gh->cgphj', cc, eye).reshape(n_chunks, gc * p, gc * S5_GROUP)

    bbd = jnp.concatenate([block_diag_in(bbr), block_diag_in(bbi)], axis=-1).astype(BF16)
    cbd = jnp.concatenate([block_diag_out(c_re), block_diag_out(-c_im)], axis=1).astype(BF16)
    lam = jnp.concatenate([lr.reshape(n_chunks, gc * p), li.reshape(n_chunks, gc * p)], axis=-1)
    lam = jnp.broadcast_to(lam[:, None, :], (n_chunks, batch, 2 * gc * p))
    return lam, bbd, cbd


def _s5_mixer(h, g, a_re, a_im, b_re, b_im, c_re, c_im, d_skip, log_step, w_glu, *, batch, tl=64):
    t, d = h.shape
    seq = t // batch
    n_chunks = d // S5_CHUNK
    ns2 = 2 * S5_CHUNK_STATES
    lam, bbd, cbd = _s5_discretize(a_re, a_im, b_re, b_im, c_re, c_im, log_step, batch)
    rows = tl * batch
    out = pl.pallas_call(
        functools.partial(_s5_kernel, tl=tl, n_chunks=n_chunks, batch=batch),
        out_shape=jax.ShapeDtypeStruct((batch, seq, d), F32),
        grid=(seq // tl,),
        in_specs=[
            pl.BlockSpec((batch, tl, d), lambda i: (0, i, 0)),
            _const_spec((1, d)),
            _const_spec((n_chunks, batch, ns2)),
            _const_spec((n_chunks, S5_CHUNK, ns2)),
            _const_spec((n_chunks, ns2, S5_CHUNK)),
            _const_spec((1, d)),
            _const_spec((d, 2 * d)),
        ],
        out_specs=pl.BlockSpec((batch, tl, d), lambda i: (0, i, 0)),
        scratch_shapes=[
            pltpu.VMEM((n_chunks, batch, ns2), F32),
            pltpu.VMEM((d // LANES, rows, LANES), F32),
        ],
        compiler_params=_params(("arbitrary",)),
        name="s5_mixer",
    )(h.reshape(batch, seq, d), g.reshape(1, d), lam, bbd, cbd, d_skip.reshape(1, d), w_glu.astype(BF16))
    return out.reshape(t, d)


def _kv_kernel(x_ref, g_ref, wk_ref, wvt_ref, k_ref, vt_ref, km_ref, *, ck):
    xn = _rms(x_ref[0], g_ref[...]).astype(BF16)
    seq, d = xn.shape
    nb = seq // MOBA_BLOCK
    for c in range(d // ck):
        cols = slice(c * ck, (c + 1) * ck)
        kc = jnp.dot(xn, wk_ref[:, cols], preferred_element_type=F32)
        k_ref[0, :, cols] = kc.astype(BF16)
        km_ref[0, :, cols] = jnp.mean(kc.reshape(nb, MOBA_BLOCK, ck), axis=1)
        vt_ref[0, cols, :] = lax.dot_general(wvt_ref[cols, :], xn, NT,
                                             preferred_element_type=F32).astype(BF16)


def _kv_proj(h, g, w_kv, *, batch, ck=256):
    t, d = h.shape
    seq = t // batch
    nb = seq // MOBA_BLOCK
    w_k = w_kv[:, :d].astype(BF16)
    w_vt = w_kv[:, d:].T.astype(BF16)
    return pl.pallas_call(
        functools.partial(_kv_kernel, ck=ck),
        out_shape=(jax.ShapeDtypeStruct((batch, seq, d), BF16),
                   jax.ShapeDtypeStruct((batch, d, seq), BF16),
                   jax.ShapeDtypeStruct((batch, nb, d), F32)),
        grid=(batch,),
        in_specs=[
            pl.BlockSpec((1, seq, d), lambda b: (b, 0, 0)),
            _const_spec((1, d)),
            _const_spec((d, d)),
            _const_spec((d, d)),
        ],
        out_specs=(pl.BlockSpec((1, seq, d), lambda b: (b, 0, 0)),
                   pl.BlockSpec((1, d, seq), lambda b: (b, 0, 0)),
                   pl.BlockSpec((1, nb, d), lambda b: (b, 0, 0))),
        compiler_params=_params(("arbitrary",)),
        name="kv_proj",
    )(h.reshape(batch, seq, d), g.reshape(1, d), w_k, w_vt)


def _q_kernel(x_ref, g_ref, w_ref, q_ref, *, scale):
    xn = _rms(x_ref[...], g_ref[...]).astype(BF16)
    q_ref[...] = (jnp.dot(xn, w_ref[...], preferred_element_type=F32) * scale).astype(BF16)


def _q_proj(h, g, w_q, *, scale, tm=512):
    t, d = h.shape
    n = w_q.shape[1]
    return pl.pallas_call(
        functools.partial(_q_kernel, scale=scale),
        out_shape=jax.ShapeDtypeStruct((t, n), BF16),
        grid=(t // tm,),
        in_specs=[pl.BlockSpec((tm, d), lambda i: (i, 0)), _const_spec((1, d)), _const_spec((d, n))],
        out_specs=pl.BlockSpec((tm, n), lambda i: (i, 0)),
        compiler_params=_params(("arbitrary",)),
        name="q_proj",
    )(h, g.reshape(1, d), w_q.astype(BF16))


def _oproj_kernel(o_ref, w_ref, h_ref, out_ref):
    out_ref[...] = h_ref[...] + jnp.dot(o_ref[...], w_ref[...], preferred_element_type=F32)


def _o_proj(o, w_o, h, *, tm=512):
    t, d = h.shape
    return pl.pallas_call(
        _oproj_kernel,
        out_shape=jax.ShapeDtypeStruct((t, d), F32),
        grid=(t // tm,),
        in_specs=[pl.BlockSpec((tm, d), lambda i: (i, 0)), _const_spec((d, d)),
                  pl.BlockSpec((tm, d), lambda i: (i, 0))],
        out_specs=pl.BlockSpec((tm, d), lambda i: (i, 0)),
        compiler_params=_params(("arbitrary",)),
        name="o_proj",
    )(o, w_o.astype(BF16), h)


def _moba_kernel(slopes_ref, q_ref, k_ref, vt_ref, km_ref, o_ref, ot_sc, *, nb, head_dim, pairs):
    bs = MOBA_BLOCK
    pair = pl.program_id(0) % pairs
    heads = LANES // head_dim
    q = q_ref[0]
    km = km_ref[0].astype(BF16)
    lane = lax.broadcasted_iota(jnp.int32, (1, LANES), 1)
    rel_t = (lax.broadcasted_iota(jnp.int32, (bs, bs), 1)
             - lax.broadcasted_iota(jnp.int32, (bs, bs), 0)).astype(F32)
    blk = lax.broadcasted_iota(jnp.int32, (nb, bs), 0)

    slope, gate_t, qb, alibi, alibi_own = [], [], [], [], []
    for hh in range(heads):
        slope.append(slopes_ref[pair * heads + hh])
        qb.append(jnp.where(lane // head_dim == hh, q, jnp.zeros_like(q)))
        gate_t.append(lax.dot_general(km, qb[hh], NT, preferred_element_type=F32))
        alibi.append(-slope[hh] * rel_t)
        alibi_own.append(jnp.where(rel_t >= 0, alibi[hh], NEG))

    def scores(i, hh):
        return lax.dot_general(k_ref[0, 0:(i + 1) * bs, :], qb[hh][i * bs:(i + 1) * bs], NT,
                               preferred_element_type=F32)

    def finish(i, hh, s):
        if i > MOBA_TOPK:
            g = gate_t[hh][:, i * bs:(i + 1) * bs]
            rank = jnp.zeros((nb, bs), jnp.int32)
            for m in range(i):
                gm = g[m:m + 1, :]
                rank = rank + ((gm > g) | ((gm == g) & (m < blk))).astype(jnp.int32)
            sel = rank < MOBA_TOPK
        t, bias, col_max = [], [], []
        for n in range(i):
            b_n = -slope[hh] * float((i - n) * bs)
            if i > MOBA_TOPK:
                b_n = jnp.where(sel[n:n + 1, :], b_n, NEG)
            t.append(s[n * bs:(n + 1) * bs] + alibi[hh])
            bias.append(b_n)
            col_max.append(jnp.max(t[n], axis=0, keepdims=True) + b_n)
        t.append(s[i * bs:] + alibi_own[hh])
        bias.append(0.0)
        m_row = jnp.max(t[i], axis=0, keepdims=True)
        for n in range(i):
            m_row = jnp.maximum(m_row, col_max[n])
        p = jnp.concatenate([jnp.exp(t[n] - (m_row - bias[n])) for n in range(i + 1)], axis=0)
        l_row = jnp.sum(p, axis=0, keepdims=True)
        o_t = jnp.dot(vt_ref[0, hh * head_dim:(hh + 1) * head_dim, 0:(i + 1) * bs], p.astype(BF16),
                      preferred_element_type=F32)
        ot_sc[hh * head_dim:(hh + 1) * head_dim, i * bs:(i + 1) * bs] = o_t / l_row

    items = [(i, hh) for i in range(nb) for hh in range(heads)]
    s_next = scores(*items[0])
    for idx, item in enumerate(items):
        s_cur = s_next
        if idx + 1 < len(items):
            s_next = scores(*items[idx + 1])
        finish(*item, s_cur)
    o_ref[0] = ot_sc[...].T.astype(BF16)


def _moba(q, k, vt, kmean, *, batch, seq, d):
    nb = seq // MOBA_BLOCK
    pairs = d // LANES
    head_dim = d // N_HEADS
    slopes = jnp.exp2(-8.0 * jnp.arange(1, N_HEADS + 1, dtype=F32) / N_HEADS)
    grid_spec = pltpu.PrefetchScalarGridSpec(
        num_scalar_prefetch=1,
        grid=(batch * pairs,),
        in_specs=[
            pl.BlockSpec((1, seq, LANES), lambda bp, s: (bp // pairs, 0, bp % pairs)),
            pl.BlockSpec((1, seq, LANES), lambda bp, s: (bp // pairs, 0, bp % pairs)),
            pl.BlockSpec((1, LANES, seq), lambda bp, s: (bp // pairs, bp % pairs, 0)),
            pl.BlockSpec((1, nb, LANES), lambda bp, s: (bp // pairs, 0, bp % pairs)),
        ],
        out_specs=pl.BlockSpec((1, seq, LANES), lambda bp, s: (bp // pairs, 0, bp % pairs)),
        scratch_shapes=[pltpu.VMEM((LANES, seq), F32)],
    )
    o = pl.pallas_call(
        functools.partial(_moba_kernel, nb=nb, head_dim=head_dim, pairs=pairs),
        out_shape=jax.ShapeDtypeStruct((batch, seq, d), BF16),
        grid_spec=grid_spec,
        compiler_params=_params(("arbitrary",)),
        name="moba",
    )(slopes, q.reshape(batch, seq, d), k, vt, kmean)
    return o.reshape(batch * seq, d)


def kernel(x, ffn1_norm, ffn1_w_in, ffn1_w_out, mix_norm, ffn2_norm, ffn2_w_in, ffn2_w_out,
           s5_a_re, s5_a_im, s5_b_re, s5_b_im, s5_c_re, s5_c_im, s5_d, s5_log_step, s5_w_glu,
           kv_norm, w_kv, w_q, w_o, final_norm):
    batch, seq, d = x.shape
    depth = ffn1_norm.shape[0]
    n_a = s5_a_re.shape[0]
    h = x.reshape(batch * seq, d)
    w1_in, w1_out = ffn1_w_in.astype(BF16), ffn1_w_out.astype(BF16)
    w2_in, w2_out = ffn2_w_in.astype(BF16), ffn2_w_out.astype(BF16)
    k = vt = kmean = None
    for layer in range(depth):
        if layer == n_a:
            k, vt, kmean = _kv_proj(h, kv_norm, w_kv, batch=batch)
        h = _ffn(h, ffn1_norm[layer], w1_in, w1_out, layer)
        if layer < n_a:
            h = _s5_mixer(h, mix_norm[layer], s5_a_re[layer], s5_a_im[layer], s5_b_re[layer],
                          s5_b_im[layer], s5_c_re[layer], s5_c_im[layer], s5_d[layer],
                          s5_log_step[layer], s5_w_glu[layer], batch=batch)
        else:
            j = layer - n_a
            q = _q_proj(h, mix_norm[layer], w_q[j], scale=(d // N_HEADS) ** -0.5)
            o = _moba(q, k, vt, kmean, batch=batch, seq=seq, d=d)
            h = _o_proj(o, w_o[j], h)
        last = layer == depth - 1
        h = _ffn(h, ffn2_norm[layer], w2_in, w2_out, layer, final_g=final_norm if last else None)
    return h.reshape(batch, seq, d)
```

```python
import functools

import jax
import jax.numpy as jnp
from jax import lax
from jax.experimental import pallas as pl
from jax.experimental.pallas import tpu as pltpu

S5_GROUP = 16
S5_STATE = 64
N_HEADS = 16
MOBA_BLOCK = 256
MOBA_TOPK = 3
EPS = 1e-6
NEG = -1e30
LOG2E = 1.4426950408889634
MOBA_AHEAD = 2

LANES = 128
S5_CHUNK = LANES
S5_CHUNK_STATES = S5_CHUNK // S5_GROUP * S5_STATE
VMEM_LIMIT = 56 * 1024 * 1024

BF16 = jnp.bfloat16
F32 = jnp.float32
NT = (((1,), (1,)), ((), ()))


def _rms(x, g):
    return x * lax.rsqrt(jnp.mean(x * x, axis=-1, keepdims=True) + EPS) * g


def _const_spec(shape, index=None):
    index = (0,) * len(shape) if index is None else index
    return pl.BlockSpec(shape, lambda *_: index, pipeline_mode=pl.Buffered(1))


def _params(semantics):
    return pltpu.CompilerParams(dimension_semantics=semantics, vmem_limit_bytes=VMEM_LIMIT)


def _ffn_kernel(x_ref, g_ref, win_ref, wout_ref, fg_ref, o_ref, *, d_ff, ck, final_norm):
    x = x_ref[...]
    xn = _rms(x, g_ref[...]).astype(BF16)
    acc = jnp.zeros(x.shape, F32)
    for c in range(d_ff // ck):
        gate = jnp.dot(xn, win_ref[:, c * ck:(c + 1) * ck], preferred_element_type=F32)
        up = jnp.dot(xn, win_ref[:, d_ff + c * ck:d_ff + (c + 1) * ck], preferred_element_type=F32)
        a = (gate * jax.nn.sigmoid(gate) * up).astype(BF16)
        acc = acc + jnp.dot(a, wout_ref[c * ck:(c + 1) * ck, :], preferred_element_type=F32)
    y = x + 0.5 * acc
    if final_norm:
        y = _rms(y, fg_ref[...])
    o_ref[...] = y


def _ffn(h, g, w_in, w_out, layer, final_g=None, *, tm=1024, ck=256):
    t, d = h.shape
    d_ff = w_out.shape[1]
    final_norm = final_g is not None
    fg = (final_g if final_norm else g).reshape(1, d)
    return pl.pallas_call(
        functools.partial(_ffn_kernel, d_ff=d_ff, ck=ck, final_norm=final_norm),
        out_shape=jax.ShapeDtypeStruct((t, d), F32),
        grid=(t // tm,),
        in_specs=[
            pl.BlockSpec((tm, d), lambda i: (i, 0)),
            _const_spec((1, d)),
            _const_spec((None, d, 2 * d_ff), (layer, 0, 0)),
            _const_spec((None, d_ff, d), (layer, 0, 0)),
            _const_spec((1, d)),
        ],
        out_specs=pl.BlockSpec((tm, d), lambda i: (i, 0)),
        compiler_params=_params(("arbitrary",)),
        name="ffn",
    )(h, g.reshape(1, d), w_in, w_out, fg)


def _to_time_major(x, slab_sc, tl, batch):
    slabs = x.shape[1] // LANES
    for s in range(slabs):
        for b in range(batch):
            slab_sc[s, pl.ds(b, tl, stride=batch), :] = x[b * tl:(b + 1) * tl, s * LANES:(s + 1) * LANES]
    return jnp.concatenate([slab_sc[s] for s in range(slabs)], axis=-1)


def _to_batch_major(y, slab_sc, tl, batch):
    slabs = y.shape[1] // LANES
    for s in range(slabs):
        slab_sc[s] = y[:, s * LANES:(s + 1) * LANES]
    return jnp.concatenate(
        [jnp.concatenate([slab_sc[s, pl.ds(b, tl, stride=batch), :] for s in range(slabs)], axis=-1)
         for b in range(batch)], axis=0)


def _s5_kernel(h_ref, g_ref, lam_ref, bbd_ref, cbd_ref, dskip_ref, wglu_ref, o_ref,
               state_sc, slab_sc, *, tl, n_chunks, batch):
    @pl.when(pl.program_id(0) == 0)
    def _():
        state_sc[...] = jnp.zeros(state_sc.shape, F32)

    ns = S5_CHUNK_STATES
    d = h_ref.shape[-1]
    h = h_ref[...].reshape(batch * tl, d)
    hn = _to_time_major(_rms(h, g_ref[...]), slab_sc, tl, batch)
    hn_b = hn.astype(BF16)
    z = []
    for j in range(n_chunks):
        cols = slice(j * S5_CHUNK, (j + 1) * S5_CHUNK)
        bu = jnp.dot(hn_b[:, cols], bbd_ref[j], preferred_element_type=F32)
        lam = lam_ref[j]
        lr, li = lam[:, :ns], lam[:, ns:]
        x = state_sc[j]
        xs = []
        for t in range(tl):
            bu_t = bu[t * batch:(t + 1) * batch]
            xr, xi = x[:, :ns], x[:, ns:]
            nr = lr * xr - li * xi + bu_t[:, :ns]
            ni = lr * xi + li * xr + bu_t[:, ns:]
            x = jnp.concatenate([nr, ni], axis=-1)
            xs.append(x)
        state_sc[j] = x
        y = jnp.dot(jnp.concatenate(xs, axis=0).astype(BF16), cbd_ref[j], preferred_element_type=F32)
        y = y + dskip_ref[:, cols] * hn[:, cols]
        z.append(jax.nn.gelu(y).astype(BF16))
    zz = jnp.dot(jnp.concatenate(z, axis=-1), wglu_ref[...], preferred_element_type=F32)
    mixed = _to_batch_major(zz[:, :d] * jax.nn.sigmoid(zz[:, d:]), slab_sc, tl, batch)
    o_ref[...] = (h + mixed).reshape(batch, tl, d)


def _s5_discretize(a_re, a_im, b_re, b_im, c_re, c_im, log_step, batch):
    g, p = a_re.shape
    gc = S5_CHUNK // S5_GROUP
    n_chunks = g // gc
    dt = jnp.exp(log_step)[:, None]
    ar = jnp.minimum(a_re, -1e-4)
    ai = a_im
    mag = jnp.exp(ar * dt)
    lr = mag * jnp.cos(ai * dt)
    li = mag * jnp.sin(ai * dt)
    nr = lr - 1.0
    den = ar * ar + ai * ai
    fr = (nr * ar + li * ai) / den
    fi = (li * ar - nr * ai) / den
    bbr = fr[..., None] * b_re - fi[..., None] * b_im
    bbi = fr[..., None] * b_im + fi[..., None] * b_re
    eye = jnp.eye(gc, dtype=F32)

    def block_diag_in(bb):
        bb = bb.reshape(n_chunks, gc, p, S5_GROUP)
        return jnp.einsum('cgpk,gh->cgkhp', bb, eye).reshape(n_chunks, gc * S5_GROUP, gc * p)

    def block_diag_out(cc):
        cc = cc.reshape(n_chunks, gc, S5_GROUP, p)
        return jnp.einsum('cgjp,gh->cgphj', cc, eye).reshape(n_chunks, gc * p, gc * S5_GROUP)

    bbd = jnp.concatenate([block_diag_in(bbr), block_diag_in(bbi)], axis=-1).astype(BF16)
    cbd = jnp.concatenate([block_diag_out(c_re), block_diag_out(-c_im)], axis=1).astype(BF16)
    lam = jnp.concatenate([lr.reshape(n_chunks, gc * p), li.reshape(n_chunks, gc * p)], axis=-1)
    lam = jnp.broadcast_to(lam[:, None, :], (n_chunks, batch, 2 * gc * p))
    return lam, bbd, cbd


def _s5_mixer(h, g, a_re, a_im, b_re, b_im, c_re, c_im, d_skip, log_step, w_glu, *, batch, tl=64):
    t, d = h.shape
    seq = t // batch
    n_chunks = d // S5_CHUNK
    ns2 = 2 * S5_CHUNK_STATES
    lam, bbd, cbd = _s5_discretize(a_re, a_im, b_re, b_im, c_re, c_im, log_step, batch)
    rows = tl * batch
    out = pl.pallas_call(
        functools.partial(_s5_kernel, tl=tl, n_chunks=n_chunks, batch=batch),
        out_shape=jax.ShapeDtypeStruct((batch, seq, d), F32),
        grid=(seq // tl,),
        in_specs=[
            pl.BlockSpec((batch, tl, d), lambda i: (0, i, 0)),
            _const_spec((1, d)),
            _const_spec((n_chunks, batch, ns2)),
            _const_spec((n_chunks, S5_CHUNK, ns2)),
            _const_spec((n_chunks, ns2, S5_CHUNK)),
            _const_spec((1, d)),
            _const_spec((d, 2 * d)),
        ],
        out_specs=pl.BlockSpec((batch, tl, d), lambda i: (0, i, 0)),
        scratch_shapes=[
            pltpu.VMEM((n_chunks, batch, ns2), F32),
            pltpu.VMEM((d // LANES, rows, LANES), F32),
        ],
        compiler_params=_params(("arbitrary",)),
        name="s5_mixer",
    )(h.reshape(batch, seq, d), g.reshape(1, d), lam, bbd, cbd, d_skip.reshape(1, d), w_glu.astype(BF16))
    return out.reshape(t, d)


def _kv_kernel(x_ref, g_ref, wk_ref, wvt_ref, k_ref, vt_ref, km_ref, *, ck):
    xn = _rms(x_ref[0], g_ref[...]).astype(BF16)
    seq, d = xn.shape
    nb = seq // MOBA_BLOCK
    for c in range(d // ck):
        cols = slice(c * ck, (c + 1) * ck)
        kc = jnp.dot(xn, wk_ref[:, cols], preferred_element_type=F32)
        k_ref[0, :, cols] = kc.astype(BF16)
        km_ref[0, :, cols] = jnp.mean(kc.reshape(nb, MOBA_BLOCK, ck), axis=1)
        vt_ref[0, cols, :] = lax.dot_general(wvt_ref[cols, :], xn, NT,
                                             preferred_element_type=F32).astype(BF16)


def _kv_proj(h, g, w_kv, *, batch, ck=256):
    t, d = h.shape
    seq = t // batch
    nb = seq // MOBA_BLOCK
    w_k = w_kv[:, :d].astype(BF16)
    w_vt = w_kv[:, d:].T.astype(BF16)
    return pl.pallas_call(
        functools.partial(_kv_kernel, ck=ck),
        out_shape=(jax.ShapeDtypeStruct((batch, seq, d), BF16),
                   jax.ShapeDtypeStruct((batch, d, seq), BF16),
                   jax.ShapeDtypeStruct((batch, nb, d), F32)),
        grid=(batch,),
        in_specs=[
            pl.BlockSpec((1, seq, d), lambda b: (b, 0, 0)),
            _const_spec((1, d)),
            _const_spec((d, d)),
            _const_spec((d, d)),
        ],
        out_specs=(pl.BlockSpec((1, seq, d), lambda b: (b, 0, 0)),
                   pl.BlockSpec((1, d, seq), lambda b: (b, 0, 0)),
                   pl.BlockSpec((1, nb, d), lambda b: (b, 0, 0))),
        compiler_params=_params(("arbitrary",)),
        name="kv_proj",
    )(h.reshape(batch, seq, d), g.reshape(1, d), w_k, w_vt)


def _q_kernel(x_ref, g_ref, w_ref, q_ref, *, scale):
    xn = _rms(x_ref[...], g_ref[...]).astype(BF16)
    q_ref[...] = (jnp.dot(xn, w_ref[...], preferred_element_type=F32) * scale).astype(BF16)


def _q_proj(h, g, w_q, *, scale, tm=512):
    t, d = h.shape
    n = w_q.shape[1]
    return pl.pallas_call(
        functools.partial(_q_kernel, scale=scale),
        out_shape=jax.ShapeDtypeStruct((t, n), BF16),
        grid=(t // tm,),
        in_specs=[pl.BlockSpec((tm, d), lambda i: (i, 0)), _const_spec((1, d)), _const_spec((d, n))],
        out_specs=pl.BlockSpec((tm, n), lambda i: (i, 0)),
        compiler_params=_params(("arbitrary",)),
        name="q_proj",
    )(h, g.reshape(1, d), w_q.astype(BF16))


def _oproj_kernel(o_ref, w_ref, h_ref, out_ref):
    out_ref[...] = h_ref[...] + jnp.dot(o_ref[...], w_ref[...], preferred_element_type=F32)


def _o_proj(o, w_o, h, *, tm=512):
    t, d = h.shape
    return pl.pallas_call(
        _oproj_kernel,
        out_shape=jax.ShapeDtypeStruct((t, d), F32),
        grid=(t // tm,),
        in_specs=[pl.BlockSpec((tm, d), lambda i: (i, 0)), _const_spec((d, d)),
                  pl.BlockSpec((tm, d), lambda i: (i, 0))],
        out_specs=pl.BlockSpec((tm, d), lambda i: (i, 0)),
        compiler_params=_params(("arbitrary",)),
        name="o_proj",
    )(o, w_o.astype(BF16), h)


def _moba_kernel(slopes_ref, q_ref, k_ref, vt_ref, km_ref, o_ref, ot_sc, t_sc, p_sc, *,
                 nb, head_dim, pairs):
    bs = MOBA_BLOCK
    pair = pl.program_id(0) % pairs
    heads = LANES // head_dim
    seq = q_ref.shape[1]
    q = q_ref[0].reshape(nb, bs, LANES)
    k = k_ref[0].reshape(nb, bs, LANES)
    km = km_ref[0].astype(BF16)
    lane = lax.broadcasted_iota(jnp.int32, (1, LANES), 1)
    pos = lax.broadcasted_iota(jnp.int32, (bs, LANES), 0).astype(F32)
    causal = (lax.broadcasted_iota(jnp.int32, (bs, bs), 1)
              >= lax.broadcasted_iota(jnp.int32, (bs, bs), 0))
    blk = lax.broadcasted_iota(jnp.int32, (nb, bs), 0)

    def split3(x):
        hi = x.astype(BF16).astype(F32)
        mid = (x - hi).astype(BF16).astype(F32)
        lo = (x - hi - mid).astype(BF16).astype(F32)
        return hi, mid, lo

    lane_b = lax.broadcasted_iota(jnp.int32, (bs, LANES), 1)

    def lanes_of(terms, base):
        out = jnp.zeros((bs, LANES), F32)
        for j, term in enumerate(terms):
            out = jnp.where(lane_b == base + j, term, out)
        return out.astype(BF16)

    slope, gate_t, q_aug, k_aug = [], [], [], []
    for hh in range(heads):
        slope.append(slopes_ref[pair * heads + hh])
        in_head = lane // head_dim == hh
        fb = ((hh + 1) % heads) * head_dim
        slope_v = jnp.full((bs, LANES), slope[hh], F32)
        q_feat = lanes_of(split3(slope_v) + split3(-slope_v * pos), fb)
        k_feat = lanes_of([pos] * 3 + [jnp.ones((bs, LANES), F32)] * 3, fb)
        q_aug.append(jnp.where(in_head, q, q_feat[None]).reshape(seq, LANES))
        k_aug.append(jnp.where(in_head, k, k_feat[None]).reshape(seq, LANES))
        gate_t.append(lax.dot_general(jnp.where(in_head, km, jnp.zeros_like(km)), q_aug[hh], NT,
                                      preferred_element_type=F32))

    def scores(i, hh, slot):
        s = lax.dot_general(k_aug[hh][0:(i + 1) * bs], q_aug[hh][i * bs:(i + 1) * bs], NT,
                            preferred_element_type=F32)
        col_max = []
        for n in range(i + 1):
            t_n = s[n * bs:(n + 1) * bs]
            if n == i:
                t_n = jnp.where(causal, t_n, NEG)
            t_sc[slot, n * bs:(n + 1) * bs, :] = t_n
            col_max.append(jnp.max(t_n, axis=0, keepdims=True))
        return col_max

    def finish(i, hh, slot, p_slot, col_max):
        if i > MOBA_TOPK:
            g = gate_t[hh][:, i * bs:(i + 1) * bs]
            rank = jnp.zeros((nb, bs), jnp.int32)
            for m in range(i):
                gm = g[m:m + 1, :]
                rank = rank + ((gm > g) | ((gm == g) & (m < blk))).astype(jnp.int32)
            sel = rank < MOBA_TOPK
        bias = []
        for n in range(i):
            b_n = -slope[hh] * float((i - n) * bs)
            if i > MOBA_TOPK:
                b_n = jnp.where(sel[n:n + 1, :], b_n, NEG)
            bias.append(b_n)
        bias.append(0.0)
        m_row = col_max[i]
        for n in range(i):
            m_row = jnp.maximum(m_row, col_max[n] + bias[n])
        l_row = jnp.zeros((1, bs), F32)
        for n in range(i + 1):
            p_n = jnp.exp2(t_sc[slot, n * bs:(n + 1) * bs, :] - (m_row - bias[n]))
            l_row = l_row + jnp.sum(p_n, axis=0, keepdims=True)
            p_sc[p_slot, n * bs:(n + 1) * bs, :] = p_n.astype(BF16)
        o_t = jnp.dot(vt_ref[0, hh * head_dim:(hh + 1) * head_dim, 0:(i + 1) * bs],
                      p_sc[p_slot, 0:(i + 1) * bs, :], preferred_element_type=F32)
        ot_sc[hh * head_dim:(hh + 1) * head_dim, i * bs:(i + 1) * bs] = o_t / l_row

    items = [(i, hh) for i in range(nb) for hh in range(heads)]
    ahead = t_sc.shape[0] - 1
    pending = [scores(*item, slot) for slot, item in enumerate(items[:ahead])]
    for idx, item in enumerate(items):
        if idx + ahead < len(items):
            pending.append(scores(*items[idx + ahead], (idx + ahead) % (ahead + 1)))
        finish(*item, idx % (ahead + 1), idx % p_sc.shape[0], pending.pop(0))
    o_ref[0] = ot_sc[...].T.astype(BF16)


def _moba(q, k, vt, kmean, *, batch, seq, d):
    nb = seq // MOBA_BLOCK
    pairs = d // LANES
    head_dim = d // N_HEADS
    slopes = LOG2E * jnp.exp2(-8.0 * jnp.arange(1, N_HEADS + 1, dtype=F32) / N_HEADS)
    grid_spec = pltpu.PrefetchScalarGridSpec(
        num_scalar_prefetch=1,
        grid=(batch * pairs,),
        in_specs=[
            pl.BlockSpec((1, seq, LANES), lambda bp, s: (bp // pairs, 0, bp % pairs)),
            pl.BlockSpec((1, seq, LANES), lambda bp, s: (bp // pairs, 0, bp % pairs)),
            pl.BlockSpec((1, LANES, seq), lambda bp, s: (bp // pairs, bp % pairs, 0)),
            pl.BlockSpec((1, nb, LANES), lambda bp, s: (bp // pairs, 0, bp % pairs)),
        ],
        out_specs=pl.BlockSpec((1, seq, LANES), lambda bp, s: (bp // pairs, 0, bp % pairs)),
        scratch_shapes=[
            pltpu.VMEM((LANES, seq), F32),
            pltpu.VMEM((MOBA_AHEAD + 1, seq, MOBA_BLOCK), F32),
            pltpu.VMEM((2, seq, MOBA_BLOCK), BF16),
        ],
    )
    o = pl.pallas_call(
        functools.partial(_moba_kernel, nb=nb, head_dim=head_dim, pairs=pairs),
        out_shape=jax.ShapeDtypeStruct((batch, seq, d), BF16),
        grid_spec=grid_spec,
        compiler_params=_params(("arbitrary",)),
        name="moba",
    )(slopes, q.reshape(batch, seq, d), k, vt, kmean)
    return o.reshape(batch * seq, d)


def kernel(x, ffn1_norm, ffn1_w_in, ffn1_w_out, mix_norm, ffn2_norm, ffn2_w_in, ffn2_w_out,
           s5_a_re, s5_a_im, s5_b_re, s5_b_im, s5_c_re, s5_c_im, s5_d, s5_log_step, s5_w_glu,
           kv_norm, w_kv, w_q, w_o, final_norm):
    batch, seq, d = x.shape
    depth = ffn1_norm.shape[0]
    n_a = s5_a_re.shape[0]
    h = x.reshape(batch * seq, d)
    w1_in, w1_out = ffn1_w_in.astype(BF16), ffn1_w_out.astype(BF16)
    w2_in, w2_out = ffn2_w_in.astype(BF16), ffn2_w_out.astype(BF16)
    k = vt = kmean = None
    for layer in range(depth):
        if layer == n_a:
            k, vt, kmean = _kv_proj(h, kv_norm, w_kv, batch=batch)
        h = _ffn(h, ffn1_norm[layer], w1_in, w1_out, layer)
        if layer < n_a:
            h = _s5_mixer(h, mix_norm[layer], s5_a_re[layer], s5_a_im[layer], s5_b_re[layer],
                          s5_b_im[layer], s5_c_re[layer], s5_c_im[layer], s5_d[layer],
                          s5_log_step[layer], s5_w_glu[layer], batch=batch)
        else:
            j = layer - n_a
            q = _q_proj(h, mix_norm[layer], w_q[j], scale=LOG2E * (d // N_HEADS) ** -0.5)
            o = _moba(q, k, vt, kmean, batch=batch, seq=seq, d=d)
            h = _o_proj(o, w_o[j], h)
        last = layer == depth - 1
        h = _ffn(h, ffn2_norm[layer], w2_in, w2_out, layer, final_g=final_norm if last else None)
    return h.reshape(batch, seq, d)
```

```python
import functools

import jax
import jax.numpy as jnp
from jax import lax
from jax.experimental import pallas as pl
from jax.experimental.pallas import tpu as pltpu

S5_GROUP = 16
S5_STATE = 64
N_HEADS = 16
MOBA_BLOCK = 256
MOBA_TOPK = 3
EPS = 1e-6
NEG = -1e30
LOG2E = 1.4426950408889634
MOBA_AHEAD = 3

LANES = 128
S5_CHUNK = LANES
S5_CHUNK_STATES = S5_CHUNK // S5_GROUP * S5_STATE
VMEM_LIMIT = 56 * 1024 * 1024

BF16 = jnp.bfloat16
F32 = jnp.float32
NT = (((1,), (1,)), ((), ()))


def _rms(x, g):
    return x * lax.rsqrt(jnp.mean(x * x, axis=-1, keepdims=True) + EPS) * g


def _const_spec(shape, index=None):
    index = (0,) * len(shape) if index is None else index
    return pl.BlockSpec(shape, lambda *_: index, pipeline_mode=pl.Buffered(1))


def _params(semantics):
    return pltpu.CompilerParams(dimension_semantics=semantics, vmem_limit_bytes=VMEM_LIMIT)


def _ffn_kernel(*refs, d_ff, ck, attn_in, q_out, final_norm, q_scale):
    refs = list(refs)
    x_ref = refs.pop(0)
    attn_ref, wo_ref = (refs.pop(0), refs.pop(0)) if attn_in else (None, None)
    g_ref, win_ref, wout_ref = refs.pop(0), refs.pop(0), refs.pop(0)
    gq_ref, wq_ref = (refs.pop(0), refs.pop(0)) if q_out else (None, None)
    fg_ref = refs.pop(0) if final_norm else None
    o_ref = refs.pop(0)
    q_ref = refs.pop(0) if q_out else None

    x = x_ref[...]
    if attn_in:
        x = x + jnp.dot(attn_ref[...], wo_ref[...], preferred_element_type=F32)
    xn = _rms(x, g_ref[...]).astype(BF16)
    acc = jnp.zeros(x.shape, F32)
    for c in range(d_ff // ck):
        gate = jnp.dot(xn, win_ref[:, c * ck:(c + 1) * ck], preferred_element_type=F32)
        up = jnp.dot(xn, win_ref[:, d_ff + c * ck:d_ff + (c + 1) * ck], preferred_element_type=F32)
        a = (gate * jax.nn.sigmoid(gate) * up).astype(BF16)
        acc = acc + jnp.dot(a, wout_ref[c * ck:(c + 1) * ck, :], preferred_element_type=F32)
    y = x + 0.5 * acc
    if q_out:
        yn = _rms(y, gq_ref[...]).astype(BF16)
        q_ref[...] = (jnp.dot(yn, wq_ref[...], preferred_element_type=F32) * q_scale).astype(BF16)
    if final_norm:
        y = _rms(y, fg_ref[...])
    o_ref[...] = y


def _ffn(h, g, w_in, w_out, layer, *, attn=None, q_proj=None, final_g=None, tm=1024, ck=256):
    t, d = h.shape
    d_ff = w_out.shape[1]
    row_spec = pl.BlockSpec((tm, d), lambda i: (i, 0))
    args, in_specs = [h], [row_spec]
    if attn is not None:
        args += [attn[0], attn[1].astype(BF16)]
        in_specs += [row_spec, _const_spec((d, d))]
    args += [g.reshape(1, d), w_in, w_out]
    in_specs += [_const_spec((1, d)), _const_spec((None, d, 2 * d_ff), (layer, 0, 0)),
                 _const_spec((None, d_ff, d), (layer, 0, 0))]
    out_shape, out_specs = [jax.ShapeDtypeStruct((t, d), F32)], [row_spec]
    if q_proj is not None:
        args += [q_proj[0].reshape(1, d), q_proj[1].astype(BF16)]
        in_specs += [_const_spec((1, d)), _const_spec((d, d))]
        out_shape.append(jax.ShapeDtypeStruct((t, d), BF16))
        out_specs.append(row_spec)
    if final_g is not None:
        args.append(final_g.reshape(1, d))
        in_specs.append(_const_spec((1, d)))
    out = pl.pallas_call(
        functools.partial(_ffn_kernel, d_ff=d_ff, ck=ck, attn_in=attn is not None, q_out=q_proj is not None,
                          final_norm=final_g is not None, q_scale=q_proj[2] if q_proj else None),
        out_shape=out_shape,
        grid=(t // tm,),
        in_specs=in_specs,
        out_specs=out_specs,
        compiler_params=_params(("arbitrary",)),
        name="ffn",
    )(*args)
    return out if q_proj is not None else out[0]


def _to_time_major(x, slab_sc, tl, batch):
    slabs = x.shape[1] // LANES
    for s in range(slabs):
        for b in range(batch):
            slab_sc[s, pl.ds(b, tl, stride=batch), :] = x[b * tl:(b + 1) * tl, s * LANES:(s + 1) * LANES]
    return jnp.concatenate([slab_sc[s] for s in range(slabs)], axis=-1)


def _to_batch_major(y, slab_sc, tl, batch):
    slabs = y.shape[1] // LANES
    for s in range(slabs):
        slab_sc[s] = y[:, s * LANES:(s + 1) * LANES]
    return jnp.concatenate(
        [jnp.concatenate([slab_sc[s, pl.ds(b, tl, stride=batch), :] for s in range(slabs)], axis=-1)
         for b in range(batch)], axis=0)


def _s5_kernel(h_ref, g_ref, lam_ref, bbd_ref, cbd_ref, dskip_ref, wglu_ref, o_ref,
               state_sc, slab_sc, *, tl, n_chunks, batch):
    @pl.when(pl.program_id(0) == 0)
    def _():
        state_sc[...] = jnp.zeros(state_sc.shape, F32)

    ns = S5_CHUNK_STATES
    d = h_ref.shape[-1]
    h = h_ref[...].reshape(batch * tl, d)
    hn = _to_time_major(_rms(h, g_ref[...]), slab_sc, tl, batch)
    hn_b = hn.astype(BF16)
    z = []
    for j in range(n_chunks):
        cols = slice(j * S5_CHUNK, (j + 1) * S5_CHUNK)
        bu = jnp.dot(hn_b[:, cols], bbd_ref[j], preferred_element_type=F32)
        lam = lam_ref[j]
        lr, li = lam[:, :ns], lam[:, ns:]
        x = state_sc[j]
        xs = []
        for t in range(tl):
            bu_t = bu[t * batch:(t + 1) * batch]
            xr, xi = x[:, :ns], x[:, ns:]
            nr = lr * xr - li * xi + bu_t[:, :ns]
            ni = lr * xi + li * xr + bu_t[:, ns:]
            x = jnp.concatenate([nr, ni], axis=-1)
            xs.append(x)
        state_sc[j] = x
        y = jnp.dot(jnp.concatenate(xs, axis=0).astype(BF16), cbd_ref[j], preferred_element_type=F32)
        y = y + dskip_ref[:, cols] * hn[:, cols]
        z.append(jax.nn.gelu(y).astype(BF16))
    zz = jnp.dot(jnp.concatenate(z, axis=-1), wglu_ref[...], preferred_element_type=F32)
    mixed = _to_batch_major(zz[:, :d] * jax.nn.sigmoid(zz[:, d:]), slab_sc, tl, batch)
    o_ref[...] = (h + mixed).reshape(batch, tl, d)


def _s5_discretize(a_re, a_im, b_re, b_im, c_re, c_im, log_step, batch):
    g, p = a_re.shape
    gc = S5_CHUNK // S5_GROUP
    n_chunks = g // gc
    dt = jnp.exp(log_step)[:, None]
    ar = jnp.minimum(a_re, -1e-4)
    ai = a_im
    mag = jnp.exp(ar * dt)
    lr = mag * jnp.cos(ai * dt)
    li = mag * jnp.sin(ai * dt)
    nr = lr - 1.0
    den = ar * ar + ai * ai
    fr = (nr * ar + li * ai) / den
    fi = (li * ar - nr * ai) / den
    bbr = fr[..., None] * b_re - fi[..., None] * b_im
    bbi = fr[..., None] * b_im + fi[..., None] * b_re
    eye = jnp.eye(gc, dtype=F32)

    def block_diag_in(bb):
        bb = bb.reshape(n_chunks, gc, p, S5_GROUP)
        return jnp.einsum('cgpk,gh->cgkhp', bb, eye).reshape(n_chunks, gc * S5_GROUP, gc * p)

    def block_diag_out(cc):
        cc = cc.reshape(n_chunks, gc, S5_GROUP, p)
        return jnp.einsum('cgjp,gh->cgphj', cc, eye).reshape(n_chunks, gc * p, gc * S5_GROUP)

    bbd = jnp.concatenate([block_diag_in(bbr), block_diag_in(bbi)], axis=-1).astype(BF16)
    cbd = jnp.concatenate([block_diag_out(c_re), block_diag_out(-c_im)], axis=1).astype(BF16)
    lam = jnp.concatenate([lr.reshape(n_chunks, gc * p), li.reshape(n_chunks, gc * p)], axis=-1)
    lam = jnp.broadcast_to(lam[:, None, :], (n_chunks, batch, 2 * gc * p))
    return lam, bbd, cbd


def _s5_mixer(h, g, a_re, a_im, b_re, b_im, c_re, c_im, d_skip, log_step, w_glu, *, batch, tl=64):
    t, d = h.shape
    seq = t // batch
    n_chunks = d // S5_CHUNK
    ns2 = 2 * S5_CHUNK_STATES
    lam, bbd, cbd = _s5_discretize(a_re, a_im, b_re, b_im, c_re, c_im, log_step, batch)
    rows = tl * batch
    out = pl.pallas_call(
        functools.partial(_s5_kernel, tl=tl, n_chunks=n_chunks, batch=batch),
        out_shape=jax.ShapeDtypeStruct((batch, seq, d), F32),
        grid=(seq // tl,),
        in_specs=[
            pl.BlockSpec((batch, tl, d), lambda i: (0, i, 0)),
            _const_spec((1, d)),
            _const_spec((n_chunks, batch, ns2)),
            _const_spec((n_chunks, S5_CHUNK, ns2)),
            _const_spec((n_chunks, ns2, S5_CHUNK)),
            _const_spec((1, d)),
            _const_spec((d, 2 * d)),
        ],
        out_specs=pl.BlockSpec((batch, tl, d), lambda i: (0, i, 0)),
        scratch_shapes=[
            pltpu.VMEM((n_chunks, batch, ns2), F32),
            pltpu.VMEM((d // LANES, rows, LANES), F32),
        ],
        compiler_params=_params(("arbitrary",)),
        name="s5_mixer",
    )(h.reshape(batch, seq, d), g.reshape(1, d), lam, bbd, cbd, d_skip.reshape(1, d), w_glu.astype(BF16))
    return out.reshape(t, d)


def _kv_kernel(x_ref, g_ref, wk_ref, wvt_ref, k_ref, vt_ref, km_ref, *, ck):
    xn = _rms(x_ref[0], g_ref[...]).astype(BF16)
    seq, d = xn.shape
    nb = seq // MOBA_BLOCK
    for c in range(d // ck):
        cols = slice(c * ck, (c + 1) * ck)
        kc = jnp.dot(xn, wk_ref[:, cols], preferred_element_type=F32)
        k_ref[0, :, cols] = kc.astype(BF16)
        km_ref[0, :, cols] = jnp.mean(kc.reshape(nb, MOBA_BLOCK, ck), axis=1)
        vt_ref[0, cols, :] = lax.dot_general(wvt_ref[cols, :], xn, NT,
                                             preferred_element_type=F32).astype(BF16)


def _kv_proj(h, g, w_kv, *, batch, ck=256):
    t, d = h.shape
    seq = t // batch
    nb = seq // MOBA_BLOCK
    w_k = w_kv[:, :d].astype(BF16)
    w_vt = w_kv[:, d:].T.astype(BF16)
    return pl.pallas_call(
        functools.partial(_kv_kernel, ck=ck),
        out_shape=(jax.ShapeDtypeStruct((batch, seq, d), BF16),
                   jax.ShapeDtypeStruct((batch, d, seq), BF16),
                   jax.ShapeDtypeStruct((batch, nb, d), F32)),
        grid=(batch,),
        in_specs=[
            pl.BlockSpec((1, seq, d), lambda b: (b, 0, 0)),
            _const_spec((1, d)),
            _const_spec((d, d)),
            _const_spec((d, d)),
        ],
        out_specs=(pl.BlockSpec((1, seq, d), lambda b: (b, 0, 0)),
                   pl.BlockSpec((1, d, seq), lambda b: (b, 0, 0)),
                   pl.BlockSpec((1, nb, d), lambda b: (b, 0, 0))),
        compiler_params=_params(("arbitrary",)),
        name="kv_proj",
    )(h.reshape(batch, seq, d), g.reshape(1, d), w_k, w_vt)


def _moba_kernel(slopes_ref, q_ref, k_ref, vt_ref, km_ref, o_ref, ot_sc, t_sc, p_sc, *,
                 nb, head_dim, pairs):
    bs = MOBA_BLOCK
    pair = pl.program_id(0) % pairs
    heads = LANES // head_dim
    seq = q_ref.shape[1]
    q = q_ref[0].reshape(nb, bs, LANES)
    k = k_ref[0].reshape(nb, bs, LANES)
    km = km_ref[0].astype(BF16)
    lane = lax.broadcasted_iota(jnp.int32, (1, LANES), 1)
    pos = lax.broadcasted_iota(jnp.int32, (bs, LANES), 0).astype(F32)
    causal = (lax.broadcasted_iota(jnp.int32, (bs, bs), 1)
              >= lax.broadcasted_iota(jnp.int32, (bs, bs), 0))
    blk = lax.broadcasted_iota(jnp.int32, (nb, bs), 0)

    def split3(x):
        hi = x.astype(BF16).astype(F32)
        mid = (x - hi).astype(BF16).astype(F32)
        lo = (x - hi - mid).astype(BF16).astype(F32)
        return hi, mid, lo

    lane_b = lax.broadcasted_iota(jnp.int32, (bs, LANES), 1)

    def lanes_of(terms, base):
        out = jnp.zeros((bs, LANES), F32)
        for j, term in enumerate(terms):
            out = jnp.where(lane_b == base + j, term, out)
        return out.astype(BF16)

    slope, gate_t, q_aug, k_aug = [], [], [], []
    for hh in range(heads):
        slope.append(slopes_ref[pair * heads + hh])
        in_head = lane // head_dim == hh
        fb = ((hh + 1) % heads) * head_dim
        slope_v = jnp.full((bs, LANES), slope[hh], F32)
        q_feat = lanes_of(split3(slope_v) + split3(-slope_v * pos), fb)
        k_feat = lanes_of([pos] * 3 + [jnp.ones((bs, LANES), F32)] * 3, fb)
        q_aug.append(jnp.where(in_head, q, q_feat[None]).reshape(seq, LANES))
        k_aug.append(jnp.where(in_head, k, k_feat[None]).reshape(seq, LANES))
        gate_t.append(lax.dot_general(jnp.where(in_head, km, jnp.zeros_like(km)), q_aug[hh], NT,
                                      preferred_element_type=F32))

    def scores(i, hh, slot):
        s = lax.dot_general(k_aug[hh][0:(i + 1) * bs], q_aug[hh][i * bs:(i + 1) * bs], NT,
                            preferred_element_type=F32)
        col_max = []
        for n in range(i + 1):
            t_n = s[n * bs:(n + 1) * bs]
            if n == i:
                t_n = jnp.where(causal, t_n, NEG)
            t_sc[slot, n * bs:(n + 1) * bs, :] = t_n
            col_max.append(jnp.max(t_n, axis=0, keepdims=True))
        return col_max

    def finish(i, hh, slot, p_slot, col_max):
        if i > MOBA_TOPK:
            g = gate_t[hh][:, i * bs:(i + 1) * bs]
            rank = jnp.zeros((nb, bs), jnp.int32)
            for m in range(i):
                gm = g[m:m + 1, :]
                rank = rank + ((gm > g) | ((gm == g) & (m < blk))).astype(jnp.int32)
            sel = rank < MOBA_TOPK
        bias = []
        for n in range(i):
            b_n = -slope[hh] * float((i - n) * bs)
            if i > MOBA_TOPK:
                b_n = jnp.where(sel[n:n + 1, :], b_n, NEG)
            bias.append(b_n)
        bias.append(0.0)
        m_row = col_max[i]
        for n in range(i):
            m_row = jnp.maximum(m_row, col_max[n] + bias[n])
        l_row = jnp.zeros((1, bs), F32)
        for n in range(i + 1):
            p_n = jnp.exp2(t_sc[slot, n * bs:(n + 1) * bs, :] - (m_row - bias[n]))
            l_row = l_row + jnp.sum(p_n, axis=0, keepdims=True)
            p_sc[p_slot, n * bs:(n + 1) * bs, :] = p_n.astype(BF16)
        o_t = jnp.dot(vt_ref[0, hh * head_dim:(hh + 1) * head_dim, 0:(i + 1) * bs],
                      p_sc[p_slot, 0:(i + 1) * bs, :], preferred_element_type=F32)
        ot_sc[hh * head_dim:(hh + 1) * head_dim, i * bs:(i + 1) * bs] = o_t / l_row
        if hh == heads - 1:
            o_ref[0, i * bs:(i + 1) * bs, :] = ot_sc[:, i * bs:(i + 1) * bs].T.astype(BF16)

    items = [(i, hh) for i in reversed(range(nb)) for hh in range(heads)]
    ahead = t_sc.shape[0] - 1
    pending = [scores(*item, slot) for slot, item in enumerate(items[:ahead])]
    for idx, item in enumerate(items):
        if idx + ahead < len(items):
            pending.append(scores(*items[idx + ahead], (idx + ahead) % (ahead + 1)))
        finish(*item, idx % (ahead + 1), idx % p_sc.shape[0], pending.pop(0))


def _moba(q, k, vt, kmean, *, batch, seq, d):
    nb = seq // MOBA_BLOCK
    pairs = d // LANES
    head_dim = d // N_HEADS
    slopes = LOG2E * jnp.exp2(-8.0 * jnp.arange(1, N_HEADS + 1, dtype=F32) / N_HEADS)
    grid_spec = pltpu.PrefetchScalarGridSpec(
        num_scalar_prefetch=1,
        grid=(batch * pairs,),
        in_specs=[
            pl.BlockSpec((1, seq, LANES), lambda bp, s: (bp // pairs, 0, bp % pairs)),
            pl.BlockSpec((1, seq, LANES), lambda bp, s: (bp // pairs, 0, bp % pairs)),
            pl.BlockSpec((1, LANES, seq), lambda bp, s: (bp // pairs, bp % pairs, 0)),
            pl.BlockSpec((1, nb, LANES), lambda bp, s: (bp // pairs, 0, bp % pairs)),
        ],
        out_specs=pl.BlockSpec((1, seq, LANES), lambda bp, s: (bp // pairs, 0, bp % pairs)),
        scratch_shapes=[
            pltpu.VMEM((LANES, seq), F32),
            pltpu.VMEM((MOBA_AHEAD + 1, seq, MOBA_BLOCK), F32),
            pltpu.VMEM((2, seq, MOBA_BLOCK), BF16),
        ],
    )
    o = pl.pallas_call(
        functools.partial(_moba_kernel, nb=nb, head_dim=head_dim, pairs=pairs),
        out_shape=jax.ShapeDtypeStruct((batch, seq, d), BF16),
        grid_spec=grid_spec,
        compiler_params=_params(("arbitrary",)),
        name="moba",
    )(slopes, q.reshape(batch, seq, d), k, vt, kmean)
    return o.reshape(batch * seq, d)


def kernel(x, ffn1_norm, ffn1_w_in, ffn1_w_out, mix_norm, ffn2_norm, ffn2_w_in, ffn2_w_out,
           s5_a_re, s5_a_im, s5_b_re, s5_b_im, s5_c_re, s5_c_im, s5_d, s5_log_step, s5_w_glu,
           kv_norm, w_kv, w_q, w_o, final_norm):
    batch, seq, d = x.shape
    depth = ffn1_norm.shape[0]
    n_a = s5_a_re.shape[0]
    h = x.reshape(batch * seq, d)
    w1_in, w1_out = ffn1_w_in.astype(BF16), ffn1_w_out.astype(BF16)
    w2_in, w2_out = ffn2_w_in.astype(BF16), ffn2_w_out.astype(BF16)
    k = vt = kmean = None
    for layer in range(depth):
        if layer == n_a:
            k, vt, kmean = _kv_proj(h, kv_norm, w_kv, batch=batch)
        if layer < n_a:
            h = _ffn(h, ffn1_norm[layer], w1_in, w1_out, layer)
            h = _s5_mixer(h, mix_norm[layer], s5_a_re[layer], s5_a_im[layer], s5_b_re[layer],
                          s5_b_im[layer], s5_c_re[layer], s5_c_im[layer], s5_d[layer],
                          s5_log_step[layer], s5_w_glu[layer], batch=batch)
            attn = None
        else:
            j = layer - n_a
            h, q = _ffn(h, ffn1_norm[layer], w1_in, w1_out, layer,
                        q_proj=(mix_norm[layer], w_q[j], LOG2E * (d // N_HEADS) ** -0.5))
            attn = (_moba(q, k, vt, kmean, batch=batch, seq=seq, d=d), w_o[j])
        last = layer == depth - 1
        h = _ffn(h, ffn2_norm[layer], w2_in, w2_out, layer, attn=attn, final_g=final_norm if last else None)
    return h.reshape(batch, seq, d)
```

```python
import functools

import jax
import jax.numpy as jnp
from jax import lax
from jax.experimental import pallas as pl
from jax.experimental.pallas import tpu as pltpu

S5_GROUP = 16
S5_STATE = 64
N_HEADS = 16
MOBA_BLOCK = 256
MOBA_TOPK = 3
EPS = 1e-6
NEG = -1e30
LOG2E = 1.4426950408889634
MOBA_AHEAD = 3

LANES = 128
S5_CHUNK = LANES
S5_CHUNK_STATES = S5_CHUNK // S5_GROUP * S5_STATE
VMEM_LIMIT = 56 * 1024 * 1024

BF16 = jnp.bfloat16
F32 = jnp.float32
NT = (((1,), (1,)), ((), ()))


def _rms(x, g):
    return x * lax.rsqrt(jnp.mean(x * x, axis=-1, keepdims=True) + EPS) * g


def _const_spec(shape, index=None):
    index = (0,) * len(shape) if index is None else index
    return pl.BlockSpec(shape, lambda *_: index, pipeline_mode=pl.Buffered(1))


def _params(semantics):
    return pltpu.CompilerParams(dimension_semantics=semantics, vmem_limit_bytes=VMEM_LIMIT)


def _ffn_kernel(*refs, d_ff, ck, attn_in, q_out, final_norm, q_scale):
    refs = list(refs)
    x_ref = refs.pop(0)
    attn_ref, wo_ref = (refs.pop(0), refs.pop(0)) if attn_in else (None, None)
    g_ref, win_ref, wout_ref = refs.pop(0), refs.pop(0), refs.pop(0)
    gq_ref, wq_ref = (refs.pop(0), refs.pop(0)) if q_out else (None, None)
    fg_ref = refs.pop(0) if final_norm else None
    o_ref = refs.pop(0)
    q_ref = refs.pop(0) if q_out else None

    x = x_ref[...]
    if attn_in:
        x = x + jnp.dot(attn_ref[...], wo_ref[...], preferred_element_type=F32)
    xn = _rms(x, g_ref[...]).astype(BF16)
    acc = jnp.zeros(x.shape, F32)
    for c in range(d_ff // ck):
        gate = jnp.dot(xn, win_ref[:, c * ck:(c + 1) * ck], preferred_element_type=F32)
        up = jnp.dot(xn, win_ref[:, d_ff + c * ck:d_ff + (c + 1) * ck], preferred_element_type=F32)
        a = (gate * jax.nn.sigmoid(gate) * up).astype(BF16)
        acc = acc + jnp.dot(a, wout_ref[c * ck:(c + 1) * ck, :], preferred_element_type=F32)
    y = x + 0.5 * acc
    if q_out:
        yn = _rms(y, gq_ref[...]).astype(BF16)
        q_ref[...] = (jnp.dot(yn, wq_ref[...], preferred_element_type=F32) * q_scale).astype(BF16)
    if final_norm:
        y = _rms(y, fg_ref[...])
    o_ref[...] = y


def _ffn(h, g, w_in, w_out, layer, *, attn=None, q_proj=None, final_g=None, tm=1024, ck=256):
    t, d = h.shape
    d_ff = w_out.shape[1]
    row_spec = pl.BlockSpec((tm, d), lambda i: (i, 0))
    args, in_specs = [h], [row_spec]
    if attn is not None:
        args += [attn[0], attn[1].astype(BF16)]
        in_specs += [row_spec, _const_spec((d, d))]
    args += [g.reshape(1, d), w_in, w_out]
    in_specs += [_const_spec((1, d)), _const_spec((None, d, 2 * d_ff), (layer, 0, 0)),
                 _const_spec((None, d_ff, d), (layer, 0, 0))]
    out_shape, out_specs = [jax.ShapeDtypeStruct((t, d), F32)], [row_spec]
    if q_proj is not None:
        args += [q_proj[0].reshape(1, d), q_proj[1].astype(BF16)]
        in_specs += [_const_spec((1, d)), _const_spec((d, d))]
        out_shape.append(jax.ShapeDtypeStruct((t, d), BF16))
        out_specs.append(row_spec)
    if final_g is not None:
        args.append(final_g.reshape(1, d))
        in_specs.append(_const_spec((1, d)))
    out = pl.pallas_call(
        functools.partial(_ffn_kernel, d_ff=d_ff, ck=ck, attn_in=attn is not None, q_out=q_proj is not None,
                          final_norm=final_g is not None, q_scale=q_proj[2] if q_proj else None),
        out_shape=out_shape,
        grid=(t // tm,),
        in_specs=in_specs,
        out_specs=out_specs,
        compiler_params=_params(("arbitrary",)),
        name="ffn",
    )(*args)
    return out if q_proj is not None else out[0]


def _to_time_major(x, slab_sc, tl, batch):
    slabs = x.shape[1] // LANES
    for s in range(slabs):
        for b in range(batch):
            slab_sc[s, pl.ds(b, tl, stride=batch), :] = x[b * tl:(b + 1) * tl, s * LANES:(s + 1) * LANES]
    return jnp.concatenate([slab_sc[s] for s in range(slabs)], axis=-1)


def _to_batch_major(y, slab_sc, tl, batch):
    slabs = y.shape[1] // LANES
    for s in range(slabs):
        slab_sc[s] = y[:, s * LANES:(s + 1) * LANES]
    return jnp.concatenate(
        [jnp.concatenate([slab_sc[s, pl.ds(b, tl, stride=batch), :] for s in range(slabs)], axis=-1)
         for b in range(batch)], axis=0)


def _s5_kernel(h_ref, g_ref, lam_ref, b2_ref, c2_ref, cb_ref, dskip_ref, wglu_ref, o_ref,
               state_sc, ynext_sc, slab_sc, *, tl, n_chunks, batch):
    @pl.when(pl.program_id(0) == 0)
    def _():
        state_sc[...] = jnp.zeros(state_sc.shape, F32)
        ynext_sc[...] = jnp.zeros(ynext_sc.shape, F32)

    ns = S5_CHUNK_STATES
    d = h_ref.shape[-1]
    half = tl // 2
    h = h_ref[...].reshape(batch * tl, d)
    hn = _to_time_major(_rms(h, g_ref[...]), slab_sc, tl, batch).reshape(half, 2 * batch, d)
    u_e = hn[:, :batch].reshape(half * batch, d)
    u_o = hn[:, batch:].reshape(half * batch, d)
    ue_b, uo_b = u_e.astype(BF16), u_o.astype(BF16)
    z_e, z_o = [], []
    for j in range(n_chunks):
        cols = slice(j * S5_CHUNK, (j + 1) * S5_CHUNK)
        bu = jnp.dot(jnp.concatenate([uo_b[:, cols], ue_b[:, cols]], axis=-1), b2_ref[j],
                     preferred_element_type=F32)
        lam = lam_ref[j]
        lr, li = lam[:, :ns], lam[:, ns:]
        x = state_sc[j]
        xs = []
        for m in range(half):
            bu_m = bu[m * batch:(m + 1) * batch]
            xr, xi = x[:, :ns], x[:, ns:]
            nr = lr * xr - li * xi + bu_m[:, :ns]
            ni = lr * xi + li * xr + bu_m[:, ns:]
            x = jnp.concatenate([nr, ni], axis=-1)
            xs.append(x)
        state_sc[j] = x
        y2 = jnp.dot(jnp.concatenate(xs, axis=0).astype(BF16), c2_ref[j], preferred_element_type=F32)
        y_o, y_next = y2[:, :S5_CHUNK], y2[:, S5_CHUNK:]
        y_e = jnp.concatenate([ynext_sc[j], y_next[:-batch]], axis=0)
        ynext_sc[j] = y_next[-batch:]
        y_e = y_e + jnp.dot(ue_b[:, cols], cb_ref[j], preferred_element_type=F32)
        z_e.append(jax.nn.gelu(y_e + dskip_ref[:, cols] * u_e[:, cols]).astype(BF16))
        z_o.append(jax.nn.gelu(y_o + dskip_ref[:, cols] * u_o[:, cols]).astype(BF16))
    z = jnp.concatenate([jnp.concatenate(z_e, axis=-1), jnp.concatenate(z_o, axis=-1)], axis=0)
    zz = jnp.dot(z, wglu_ref[...], preferred_element_type=F32)
    mixed = zz[:, :d] * jax.nn.sigmoid(zz[:, d:])
    mixed = jnp.concatenate([mixed[:half * batch].reshape(half, batch, d),
                             mixed[half * batch:].reshape(half, batch, d)], axis=1).reshape(tl * batch, d)
    o_ref[...] = (h + _to_batch_major(mixed, slab_sc, tl, batch)).reshape(batch, tl, d)


def _s5_discretize(a_re, a_im, b_re, b_im, c_re, c_im, log_step, batch):
    g, p = a_re.shape
    gc = S5_CHUNK // S5_GROUP
    n_chunks = g // gc
    dt = jnp.exp(log_step)[:, None]
    ar = jnp.minimum(a_re, -1e-4)
    ai = a_im
    mag = jnp.exp(ar * dt)
    lr = mag * jnp.cos(ai * dt)
    li = mag * jnp.sin(ai * dt)
    nr = lr - 1.0
    den = ar * ar + ai * ai
    fr = (nr * ar + li * ai) / den
    fi = (li * ar - nr * ai) / den
    bbr = fr[..., None] * b_re - fi[..., None] * b_im
    bbi = fr[..., None] * b_im + fi[..., None] * b_re
    lbr = lr[..., None] * bbr - li[..., None] * bbi
    lbi = lr[..., None] * bbi + li[..., None] * bbr
    clr = c_re * lr[:, None, :] - c_im * li[:, None, :]
    cli = c_re * li[:, None, :] + c_im * lr[:, None, :]
    cb = jnp.einsum('gjp,gpk->gkj', c_re, bbr) - jnp.einsum('gjp,gpk->gkj', c_im, bbi)
    lr2, li2 = lr * lr - li * li, 2.0 * lr * li
    eye = jnp.eye(gc, dtype=F32)

    def block_diag_in(bb):
        bb = bb.reshape(n_chunks, gc, p, S5_GROUP)
        return jnp.einsum('cgpk,gh->cgkhp', bb, eye).reshape(n_chunks, gc * S5_GROUP, gc * p)

    def block_diag_out(cc):
        cc = cc.reshape(n_chunks, gc, S5_GROUP, p)
        return jnp.einsum('cgjp,gh->cgphj', cc, eye).reshape(n_chunks, gc * p, gc * S5_GROUP)

    b2 = jnp.concatenate([jnp.concatenate([block_diag_in(bbr), block_diag_in(bbi)], axis=-1),
                          jnp.concatenate([block_diag_in(lbr), block_diag_in(lbi)], axis=-1)], axis=1)
    c2 = jnp.concatenate([jnp.concatenate([block_diag_out(c_re), block_diag_out(-c_im)], axis=1),
                          jnp.concatenate([block_diag_out(clr), block_diag_out(-cli)], axis=1)], axis=-1)
    cb = jnp.einsum('cgkj,gh->cgkhj', cb.reshape(n_chunks, gc, S5_GROUP, S5_GROUP), eye)
    cb = cb.reshape(n_chunks, S5_CHUNK, S5_CHUNK)
    lam2 = jnp.concatenate([lr2.reshape(n_chunks, gc * p), li2.reshape(n_chunks, gc * p)], axis=-1)
    lam2 = jnp.broadcast_to(lam2[:, None, :], (n_chunks, batch, 2 * gc * p))
    return lam2, b2.astype(BF16), c2.astype(BF16), cb.astype(BF16)


def _s5_mixer(h, g, a_re, a_im, b_re, b_im, c_re, c_im, d_skip, log_step, w_glu, *, batch, tl=64):
    t, d = h.shape
    seq = t // batch
    n_chunks = d // S5_CHUNK
    ns2 = 2 * S5_CHUNK_STATES
    lam2, b2, c2, cb = _s5_discretize(a_re, a_im, b_re, b_im, c_re, c_im, log_step, batch)
    rows = tl * batch
    out = pl.pallas_call(
        functools.partial(_s5_kernel, tl=tl, n_chunks=n_chunks, batch=batch),
        out_shape=jax.ShapeDtypeStruct((batch, seq, d), F32),
        grid=(seq // tl,),
        in_specs=[
            pl.BlockSpec((batch, tl, d), lambda i: (0, i, 0)),
            _const_spec((1, d)),
            _const_spec((n_chunks, batch, ns2)),
            _const_spec((n_chunks, 2 * S5_CHUNK, ns2)),
            _const_spec((n_chunks, ns2, 2 * S5_CHUNK)),
            _const_spec((n_chunks, S5_CHUNK, S5_CHUNK)),
            _const_spec((1, d)),
            _const_spec((d, 2 * d)),
        ],
        out_specs=pl.BlockSpec((batch, tl, d), lambda i: (0, i, 0)),
        scratch_shapes=[
            pltpu.VMEM((n_chunks, batch, ns2), F32),
            pltpu.VMEM((n_chunks, batch, S5_CHUNK), F32),
            pltpu.VMEM((d // LANES, rows, LANES), F32),
        ],
        compiler_params=_params(("arbitrary",)),
        name="s5_mixer",
    )(h.reshape(batch, seq, d), g.reshape(1, d), lam2, b2, c2, cb, d_skip.reshape(1, d), w_glu.astype(BF16))
    return out.reshape(t, d)


def _kv_kernel(x_ref, g_ref, wk_ref, wvt_ref, k_ref, vt_ref, km_ref, *, ck):
    xn = _rms(x_ref[0], g_ref[...]).astype(BF16)
    seq, d = xn.shape
    nb = seq // MOBA_BLOCK
    for c in range(d // ck):
        cols = slice(c * ck, (c + 1) * ck)
        kc = jnp.dot(xn, wk_ref[:, cols], preferred_element_type=F32)
        k_ref[0, :, cols] = kc.astype(BF16)
        km_ref[0, :, cols] = jnp.mean(kc.reshape(nb, MOBA_BLOCK, ck), axis=1)
        vt_ref[0, cols, :] = lax.dot_general(wvt_ref[cols, :], xn, NT,
                                             preferred_element_type=F32).astype(BF16)


def _kv_proj(h, g, w_kv, *, batch, ck=256):
    t, d = h.shape
    seq = t // batch
    nb = seq // MOBA_BLOCK
    w_k = w_kv[:, :d].astype(BF16)
    w_vt = w_kv[:, d:].T.astype(BF16)
    return pl.pallas_call(
        functools.partial(_kv_kernel, ck=ck),
        out_shape=(jax.ShapeDtypeStruct((batch, seq, d), BF16),
                   jax.ShapeDtypeStruct((batch, d, seq), BF16),
                   jax.ShapeDtypeStruct((batch, nb, d), F32)),
        grid=(batch,),
        in_specs=[
            pl.BlockSpec((1, seq, d), lambda b: (b, 0, 0)),
            _const_spec((1, d)),
            _const_spec((d, d)),
            _const_spec((d, d)),
        ],
        out_specs=(pl.BlockSpec((1, seq, d), lambda b: (b, 0, 0)),
                   pl.BlockSpec((1, d, seq), lambda b: (b, 0, 0)),
                   pl.BlockSpec((1, nb, d), lambda b: (b, 0, 0))),
        compiler_params=_params(("arbitrary",)),
        name="kv_proj",
    )(h.reshape(batch, seq, d), g.reshape(1, d), w_k, w_vt)


def _moba_kernel(slopes_ref, q_ref, k_ref, vt_ref, km_ref, o_ref, ot_sc, t_sc, p_sc, *,
                 nb, head_dim, pairs):
    bs = MOBA_BLOCK
    pair = pl.program_id(0) % pairs
    heads = LANES // head_dim
    seq = q_ref.shape[1]
    q = q_ref[0].reshape(nb, bs, LANES)
    k = k_ref[0].reshape(nb, bs, LANES)
    km = km_ref[0].astype(BF16)
    lane = lax.broadcasted_iota(jnp.int32, (1, LANES), 1)
    pos = lax.broadcasted_iota(jnp.int32, (bs, LANES), 0).astype(F32)
    causal = (lax.broadcasted_iota(jnp.int32, (bs, bs), 1)
              >= lax.broadcasted_iota(jnp.int32, (bs, bs), 0))
    blk = lax.broadcasted_iota(jnp.int32, (nb, bs), 0)

    def split3(x):
        hi = x.astype(BF16).astype(F32)
        mid = (x - hi).astype(BF16).astype(F32)
        lo = (x - hi - mid).astype(BF16).astype(F32)
        return hi, mid, lo

    lane_b = lax.broadcasted_iota(jnp.int32, (bs, LANES), 1)

    def lanes_of(terms, base):
        out = jnp.zeros((bs, LANES), F32)
        for j, term in enumerate(terms):
            out = jnp.where(lane_b == base + j, term, out)
        return out.astype(BF16)

    slope, gate_t, q_aug, k_aug = [], [], [], []
    for hh in range(heads):
        slope.append(slopes_ref[pair * heads + hh])
        in_head = lane // head_dim == hh
        fb = ((hh + 1) % heads) * head_dim
        slope_v = jnp.full((bs, LANES), slope[hh], F32)
        q_feat = lanes_of(split3(slope_v) + split3(-slope_v * pos), fb)
        k_feat = lanes_of([pos] * 3 + [jnp.ones((bs, LANES), F32)] * 3, fb)
        q_aug.append(jnp.where(in_head, q, q_feat[None]).reshape(seq, LANES))
        k_aug.append(jnp.where(in_head, k, k_feat[None]).reshape(seq, LANES))
        gate_t.append(lax.dot_general(jnp.where(in_head, km, jnp.zeros_like(km)), q_aug[hh], NT,
                                      preferred_element_type=F32))

    def scores(i, hh, slot):
        s = lax.dot_general(k_aug[hh][0:(i + 1) * bs], q_aug[hh][i * bs:(i + 1) * bs], NT,
                            preferred_element_type=F32)
        col_max = []
        for n in range(i + 1):
            t_n = s[n * bs:(n + 1) * bs]
            if n == i:
                t_n = jnp.where(causal, t_n, NEG)
            t_sc[slot, n * bs:(n + 1) * bs, :] = t_n
            col_max.append(jnp.max(t_n, axis=0, keepdims=True))
        return col_max

    def finish(i, hh, slot, p_slot, col_max):
        if i > MOBA_TOPK:
            g = gate_t[hh][:, i * bs:(i + 1) * bs]
            rank = jnp.zeros((nb, bs), jnp.int32)
            for m in range(i):
                gm = g[m:m + 1, :]
                rank = rank + ((gm > g) | ((gm == g) & (m < blk))).astype(jnp.int32)
            sel = rank < MOBA_TOPK
        bias = []
        for n in range(i):
            b_n = -slope[hh] * float((i - n) * bs)
            if i > MOBA_TOPK:
                b_n = jnp.where(sel[n:n + 1, :], b_n, NEG)
            bias.append(b_n)
        bias.append(0.0)
        m_row = col_max[i]
        for n in range(i):
            m_row = jnp.maximum(m_row, col_max[n] + bias[n])
        l_row = jnp.zeros((1, bs), F32)
        for n in range(i + 1):
            p_n = jnp.exp2(t_sc[slot, n * bs:(n + 1) * bs, :] - (m_row - bias[n]))
            l_row = l_row + jnp.sum(p_n, axis=0, keepdims=True)
            p_sc[p_slot, n * bs:(n + 1) * bs, :] = p_n.astype(BF16)
        o_t = jnp.dot(vt_ref[0, hh * head_dim:(hh + 1) * head_dim, 0:(i + 1) * bs],
                      p_sc[p_slot, 0:(i + 1) * bs, :], preferred_element_type=F32)
        ot_sc[hh * head_dim:(hh + 1) * head_dim, i * bs:(i + 1) * bs] = o_t / l_row
        if hh == heads - 1:
            o_ref[0, i * bs:(i + 1) * bs, :] = ot_sc[:, i * bs:(i + 1) * bs].T.astype(BF16)

    items = [(i, hh) for i in reversed(range(nb)) for hh in range(heads)]
    ahead = t_sc.shape[0] - 1
    pending = [scores(*item, slot) for slot, item in enumerate(items[:ahead])]
    for idx, item in enumerate(items):
        if idx + ahead < len(items):
            pending.append(scores(*items[idx + ahead], (idx + ahead) % (ahead + 1)))
        finish(*item, idx % (ahead + 1), idx % p_sc.shape[0], pending.pop(0))


def _moba(q, k, vt, kmean, *, batch, seq, d):
    nb = seq // MOBA_BLOCK
    pairs = d // LANES
    head_dim = d // N_HEADS
    slopes = LOG2E * jnp.exp2(-8.0 * jnp.arange(1, N_HEADS + 1, dtype=F32) / N_HEADS)
    grid_spec = pltpu.PrefetchScalarGridSpec(
        num_scalar_prefetch=1,
        grid=(batch * pairs,),
        in_specs=[
            pl.BlockSpec((1, seq, LANES), lambda bp, s: (bp // pairs, 0, bp % pairs)),
            pl.BlockSpec((1, seq, LANES), lambda bp, s: (bp // pairs, 0, bp % pairs)),
            pl.BlockSpec((1, LANES, seq), lambda bp, s: (bp // pairs, bp % pairs, 0)),
            pl.BlockSpec((1, nb, LANES), lambda bp, s: (bp // pairs, 0, bp % pairs)),
        ],
        out_specs=pl.BlockSpec((1, seq, LANES), lambda bp, s: (bp // pairs, 0, bp % pairs)),
        scratch_shapes=[
            pltpu.VMEM((LANES, seq), F32),
            pltpu.VMEM((MOBA_AHEAD + 1, seq, MOBA_BLOCK), F32),
            pltpu.VMEM((2, seq, MOBA_BLOCK), BF16),
        ],
    )
    o = pl.pallas_call(
        functools.partial(_moba_kernel, nb=nb, head_dim=head_dim, pairs=pairs),
        out_shape=jax.ShapeDtypeStruct((batch, seq, d), BF16),
        grid_spec=grid_spec,
        compiler_params=_params(("arbitrary",)),
        name="moba",
    )(slopes, q.reshape(batch, seq, d), k, vt, kmean)
    return o.reshape(batch * seq, d)


def kernel(x, ffn1_norm, ffn1_w_in, ffn1_w_out, mix_norm, ffn2_norm, ffn2_w_in, ffn2_w_out,
           s5_a_re, s5_a_im, s5_b_re, s5_b_im, s5_c_re, s5_c_im, s5_d, s5_log_step, s5_w_glu,
           kv_norm, w_kv, w_q, w_o, final_norm):
    batch, seq, d = x.shape
    depth = ffn1_norm.shape[0]
    n_a = s5_a_re.shape[0]
    h = x.reshape(batch * seq, d)
    w1_in, w1_out = ffn1_w_in.astype(BF16), ffn1_w_out.astype(BF16)
    w2_in, w2_out = ffn2_w_in.astype(BF16), ffn2_w_out.astype(BF16)
    k = vt = kmean = None
    for layer in range(depth):
        if layer == n_a:
            k, vt, kmean = _kv_proj(h, kv_norm, w_kv, batch=batch)
        if layer < n_a:
            h = _ffn(h, ffn1_norm[layer], w1_in, w1_out, layer)
            h = _s5_mixer(h, mix_norm[layer], s5_a_re[layer], s5_a_im[layer], s5_b_re[layer],
                          s5_b_im[layer], s5_c_re[layer], s5_c_im[layer], s5_d[layer],
                          s5_log_step[layer], s5_w_glu[layer], batch=batch)
            attn = None
        else:
            j = layer - n_a
            h, q = _ffn(h, ffn1_norm[layer], w1_in, w1_out, layer,
                        q_proj=(mix_norm[layer], w_q[j], LOG2E * (d // N_HEADS) ** -0.5))
            attn = (_moba(q, k, vt, kmean, batch=batch, seq=seq, d=d), w_o[j])
        last = layer == depth - 1
        h = _ffn(h, ffn2_norm[layer], w2_in, w2_out, layer, attn=attn, final_g=final_norm if last else None)
    return h.reshape(batch, seq, d)
```

```python
import functools

import jax
import jax.numpy as jnp
from jax import lax
from jax.experimental import pallas as pl
from jax.experimental.pallas import tpu as pltpu

S5_GROUP = 16
S5_STATE = 64
N_HEADS = 16
MOBA_BLOCK = 256
MOBA_TOPK = 3
EPS = 1e-6
NEG = -1e30
LOG2E = 1.4426950408889634
MOBA_AHEAD = 3

LANES = 128
S5_CHUNK = LANES
S5_CHUNK_STATES = S5_CHUNK // S5_GROUP * S5_STATE
VMEM_LIMIT = 56 * 1024 * 1024

BF16 = jnp.bfloat16
F32 = jnp.float32
NT = (((1,), (1,)), ((), ()))


def _rms(x, g):
    return x * lax.rsqrt(jnp.mean(x * x, axis=-1, keepdims=True) + EPS) * g


def _const_spec(shape, index=None):
    index = (0,) * len(shape) if index is None else index
    return pl.BlockSpec(shape, lambda *_: index, pipeline_mode=pl.Buffered(1))


def _params(semantics):
    return pltpu.CompilerParams(dimension_semantics=semantics, vmem_limit_bytes=VMEM_LIMIT)


def _ffn_kernel(*refs, d_ff, ck, attn_in, q_out, final_norm, q_scale):
    refs = list(refs)
    x_ref = refs.pop(0)
    attn_ref, wo_ref = (refs.pop(0), refs.pop(0)) if attn_in else (None, None)
    g_ref, win_ref, wout_ref = refs.pop(0), refs.pop(0), refs.pop(0)
    gq_ref, wq_ref = (refs.pop(0), refs.pop(0)) if q_out else (None, None)
    fg_ref = refs.pop(0) if final_norm else None
    o_ref = refs.pop(0)
    q_ref = refs.pop(0) if q_out else None

    x = x_ref[...]
    if attn_in:
        x = x + jnp.dot(attn_ref[...], wo_ref[...], preferred_element_type=F32)
    xn = _rms(x, g_ref[...]).astype(BF16)
    acc = jnp.zeros(x.shape, F32)
    for c in range(d_ff // ck):
        gate = jnp.dot(xn, win_ref[:, c * ck:(c + 1) * ck], preferred_element_type=F32)
        up = jnp.dot(xn, win_ref[:, d_ff + c * ck:d_ff + (c + 1) * ck], preferred_element_type=F32)
        a = (gate * jax.nn.sigmoid(gate) * up).astype(BF16)
        acc = acc + jnp.dot(a, wout_ref[c * ck:(c + 1) * ck, :], preferred_element_type=F32)
    y = x + 0.5 * acc
    if q_out:
        yn = _rms(y, gq_ref[...]).astype(BF16)
        q_ref[...] = (jnp.dot(yn, wq_ref[...], preferred_element_type=F32) * q_scale).astype(BF16)
    if final_norm:
        y = _rms(y, fg_ref[...])
    o_ref[...] = y


def _ffn(h, g, w_in, w_out, layer, *, attn=None, q_proj=None, final_g=None, tm=1024, ck=256):
    t, d = h.shape
    d_ff = w_out.shape[1]
    row_spec = pl.BlockSpec((tm, d), lambda i: (i, 0))
    args, in_specs = [h], [row_spec]
    if attn is not None:
        args += [attn[0], attn[1].astype(BF16)]
        in_specs += [row_spec, _const_spec((d, d))]
    args += [g.reshape(1, d), w_in, w_out]
    in_specs += [_const_spec((1, d)), _const_spec((None, d, 2 * d_ff), (layer, 0, 0)),
                 _const_spec((None, d_ff, d), (layer, 0, 0))]
    out_shape, out_specs = [jax.ShapeDtypeStruct((t, d), F32)], [row_spec]
    if q_proj is not None:
        args += [q_proj[0].reshape(1, d), q_proj[1].astype(BF16)]
        in_specs += [_const_spec((1, d)), _const_spec((d, d))]
        out_shape.append(jax.ShapeDtypeStruct((t, d), BF16))
        out_specs.append(row_spec)
    if final_g is not None:
        args.append(final_g.reshape(1, d))
        in_specs.append(_const_spec((1, d)))
    out = pl.pallas_call(
        functools.partial(_ffn_kernel, d_ff=d_ff, ck=ck, attn_in=attn is not None, q_out=q_proj is not None,
                          final_norm=final_g is not None, q_scale=q_proj[2] if q_proj else None),
        out_shape=out_shape,
        grid=(t // tm,),
        in_specs=in_specs,
        out_specs=out_specs,
        compiler_params=_params(("arbitrary",)),
        name="ffn",
    )(*args)
    return out if q_proj is not None else out[0]


def _to_time_major(x, slab_sc, tl, batch):
    slabs = x.shape[1] // LANES
    for s in range(slabs):
        for b in range(batch):
            slab_sc[s, pl.ds(b, tl, stride=batch), :] = x[b * tl:(b + 1) * tl, s * LANES:(s + 1) * LANES]
    return jnp.concatenate([slab_sc[s] for s in range(slabs)], axis=-1)


def _to_batch_major(y, slab_sc, tl, batch):
    slabs = y.shape[1] // LANES
    for s in range(slabs):
        slab_sc[s] = y[:, s * LANES:(s + 1) * LANES]
    return jnp.concatenate(
        [jnp.concatenate([slab_sc[s, pl.ds(b, tl, stride=batch), :] for s in range(slabs)], axis=-1)
         for b in range(batch)], axis=0)


def _s5_kernel(h_ref, hprev_ref, g_ref, lam_ref, b2_ref, c2_ref, cb_ref, dskip_ref, wglu_ref, o_ref,
               state_sc, ynext_sc, z_sc, slab_in_sc, slab_out_sc, *, tl, n_chunks, batch):
    @pl.when(pl.program_id(0) == 0)
    def _():
        state_sc[...] = jnp.zeros(state_sc.shape, F32)
        ynext_sc[...] = jnp.zeros(ynext_sc.shape, F32)
        z_sc[...] = jnp.zeros(z_sc.shape, BF16)

    ns = S5_CHUNK_STATES
    d = h_ref.shape[-1]
    half = tl // 2
    glu_cols = 2 * d // n_chunks
    z_prev = z_sc[...]
    hn = _to_time_major(_rms(h_ref[...].reshape(batch * tl, d), g_ref[...]), slab_in_sc, tl, batch)
    hn = hn.reshape(half, 2 * batch, d)
    u_e = hn[:, :batch].reshape(half * batch, d)
    u_o = hn[:, batch:].reshape(half * batch, d)
    ue_b, uo_b = u_e.astype(BF16), u_o.astype(BF16)
    def input_proj(j):
        cols = slice(j * S5_CHUNK, (j + 1) * S5_CHUNK)
        return jnp.dot(jnp.concatenate([uo_b[:, cols], ue_b[:, cols]], axis=-1), b2_ref[j],
                       preferred_element_type=F32)

    z_e, z_o, zz = [], [], []
    bu_next = input_proj(0)
    for j in range(n_chunks):
        cols = slice(j * S5_CHUNK, (j + 1) * S5_CHUNK)
        bu = bu_next
        if j + 1 < n_chunks:
            bu_next = input_proj(j + 1)
        zz.append(jnp.dot(z_prev, wglu_ref[:, j * glu_cols:(j + 1) * glu_cols], preferred_element_type=F32))
        lam = lam_ref[j]
        lr, li = lam[:, :ns], lam[:, ns:]
        x = state_sc[j]
        xs = []
        for m in range(half):
            bu_m = bu[m * batch:(m + 1) * batch]
            xr, xi = x[:, :ns], x[:, ns:]
            nr = lr * xr - li * xi + bu_m[:, :ns]
            ni = lr * xi + li * xr + bu_m[:, ns:]
            x = jnp.concatenate([nr, ni], axis=-1)
            xs.append(x)
        state_sc[j] = x
        y2 = jnp.dot(jnp.concatenate(xs, axis=0).astype(BF16), c2_ref[j], preferred_element_type=F32)
        y_o, y_next = y2[:, :S5_CHUNK], y2[:, S5_CHUNK:]
        y_e = jnp.concatenate([ynext_sc[j], y_next[:-batch]], axis=0)
        ynext_sc[j] = y_next[-batch:]
        y_e = y_e + jnp.dot(ue_b[:, cols], cb_ref[j], preferred_element_type=F32)
        z_e.append(jax.nn.gelu(y_e + dskip_ref[:, cols] * u_e[:, cols]).astype(BF16))
        z_o.append(jax.nn.gelu(y_o + dskip_ref[:, cols] * u_o[:, cols]).astype(BF16))
    z_sc[...] = jnp.concatenate([jnp.concatenate(z_e, axis=-1), jnp.concatenate(z_o, axis=-1)], axis=0)
    zz = jnp.concatenate(zz, axis=-1)
    mixed = zz[:, :d] * jax.nn.sigmoid(zz[:, d:])
    mixed = jnp.concatenate([mixed[:half * batch].reshape(half, batch, d),
                             mixed[half * batch:].reshape(half, batch, d)], axis=1).reshape(tl * batch, d)
    mixed = _to_batch_major(mixed, slab_out_sc, tl, batch)
    o_ref[...] = hprev_ref[...] + mixed.reshape(batch, tl, d)


def _s5_discretize(a_re, a_im, b_re, b_im, c_re, c_im, log_step, batch):
    n, g, p = a_re.shape
    gc = S5_CHUNK // S5_GROUP
    n_chunks = g // gc
    dt = jnp.exp(log_step)[..., None]
    ar = jnp.minimum(a_re, -1e-4)
    ai = a_im
    mag = jnp.exp(ar * dt)
    lr = mag * jnp.cos(ai * dt)
    li = mag * jnp.sin(ai * dt)
    nr = lr - 1.0
    den = ar * ar + ai * ai
    fr = (nr * ar + li * ai) / den
    fi = (li * ar - nr * ai) / den
    bbr = fr[..., None] * b_re - fi[..., None] * b_im
    bbi = fr[..., None] * b_im + fi[..., None] * b_re
    lbr = lr[..., None] * bbr - li[..., None] * bbi
    lbi = lr[..., None] * bbi + li[..., None] * bbr
    clr = c_re * lr[:, :, None, :] - c_im * li[:, :, None, :]
    cli = c_re * li[:, :, None, :] + c_im * lr[:, :, None, :]
    cb = jnp.einsum('ngjp,ngpk->ngkj', c_re, bbr) - jnp.einsum('ngjp,ngpk->ngkj', c_im, bbi)
    eye = jnp.eye(gc, dtype=F32)
    xb = jnp.stack([jnp.stack([bbr, bbi]), jnp.stack([lbr, lbi])]).reshape(2, 2, n, n_chunks, gc, p, S5_GROUP)
    b2 = jnp.einsum('rqncgpk,gh->ncrgkqhp', xb, eye).reshape(n, n_chunks, 2 * S5_CHUNK, 2 * gc * p)
    xc = jnp.stack([jnp.stack([c_re, clr]), jnp.stack([-c_im, -cli])]).reshape(2, 2, n, n_chunks, gc, S5_GROUP, p)
    c2 = jnp.einsum('qsncgjp,gh->ncqgpshj', xc, eye).reshape(n, n_chunks, 2 * gc * p, 2 * S5_CHUNK)
    cb = jnp.einsum('ncgkj,gh->ncgkhj', cb.reshape(n, n_chunks, gc, S5_GROUP, S5_GROUP), eye)
    cb = cb.reshape(n, n_chunks, S5_CHUNK, S5_CHUNK)
    lam2 = jnp.concatenate([(lr * lr - li * li).reshape(n, n_chunks, 1, gc * p),
                            (2.0 * lr * li).reshape(n, n_chunks, 1, gc * p)], axis=-1)
    lam2 = jnp.broadcast_to(lam2, (n, n_chunks, batch, 2 * gc * p))
    return lam2, b2.astype(BF16), c2.astype(BF16), cb.astype(BF16)


def _s5_mixer(h, g, s5_params, d_skip, w_glu, layer, *, batch, tl=64):
    t, d = h.shape
    seq = t // batch
    n_chunks = d // S5_CHUNK
    ns2 = 2 * S5_CHUNK_STATES
    rows = tl * batch
    nt = seq // tl
    h3 = h.reshape(batch, seq, d)
    out = pl.pallas_call(
        functools.partial(_s5_kernel, tl=tl, n_chunks=n_chunks, batch=batch),
        out_shape=jax.ShapeDtypeStruct((batch, seq, d), F32),
        grid=(nt + 1,),
        in_specs=[
            pl.BlockSpec((batch, tl, d), lambda i: (0, jnp.minimum(i, nt - 1), 0)),
            pl.BlockSpec((batch, tl, d), lambda i: (0, jnp.maximum(i - 1, 0), 0)),
            _const_spec((1, d)),
            _const_spec((None, n_chunks, batch, ns2), (layer, 0, 0, 0)),
            _const_spec((None, n_chunks, 2 * S5_CHUNK, ns2), (layer, 0, 0, 0)),
            _const_spec((None, n_chunks, ns2, 2 * S5_CHUNK), (layer, 0, 0, 0)),
            _const_spec((None, n_chunks, S5_CHUNK, S5_CHUNK), (layer, 0, 0, 0)),
            _const_spec((1, d)),
            _const_spec((None, d, 2 * d), (layer, 0, 0)),
        ],
        out_specs=pl.BlockSpec((batch, tl, d), lambda i: (0, jnp.maximum(i - 1, 0), 0)),
        scratch_shapes=[
            pltpu.VMEM((n_chunks, batch, ns2), F32),
            pltpu.VMEM((n_chunks, batch, S5_CHUNK), F32),
            pltpu.VMEM((rows, d), BF16),
            pltpu.VMEM((d // LANES, rows, LANES), F32),
            pltpu.VMEM((d // LANES, rows, LANES), F32),
        ],
        compiler_params=_params(("arbitrary",)),
        name="s5_mixer",
    )(h3, h3, g.reshape(1, d), *s5_params, d_skip.reshape(1, d), w_glu)
    return out.reshape(t, d)


def _kv_kernel(x_ref, g_ref, wk_ref, wvt_ref, k_ref, vt_ref, km_ref, *, ck):
    xn = _rms(x_ref[0], g_ref[...]).astype(BF16)
    seq, d = xn.shape
    nb = seq // MOBA_BLOCK
    for c in range(d // ck):
        cols = slice(c * ck, (c + 1) * ck)
        kc = jnp.dot(xn, wk_ref[:, cols], preferred_element_type=F32)
        k_ref[0, :, cols] = kc.astype(BF16)
        km_ref[0, :, cols] = jnp.mean(kc.reshape(nb, MOBA_BLOCK, ck), axis=1)
        vt_ref[0, cols, :] = lax.dot_general(wvt_ref[cols, :], xn, NT,
                                             preferred_element_type=F32).astype(BF16)


def _kv_proj(h, g, w_kv, *, batch, ck=256):
    t, d = h.shape
    seq = t // batch
    nb = seq // MOBA_BLOCK
    w_k = w_kv[:, :d].astype(BF16)
    w_vt = w_kv[:, d:].T.astype(BF16)
    return pl.pallas_call(
        functools.partial(_kv_kernel, ck=ck),
        out_shape=(jax.ShapeDtypeStruct((batch, seq, d), BF16),
                   jax.ShapeDtypeStruct((batch, d, seq), BF16),
                   jax.ShapeDtypeStruct((batch, nb, d), F32)),
        grid=(batch,),
        in_specs=[
            pl.BlockSpec((1, seq, d), lambda b: (b, 0, 0)),
            _const_spec((1, d)),
            _const_spec((d, d)),
            _const_spec((d, d)),
        ],
        out_specs=(pl.BlockSpec((1, seq, d), lambda b: (b, 0, 0)),
                   pl.BlockSpec((1, d, seq), lambda b: (b, 0, 0)),
                   pl.BlockSpec((1, nb, d), lambda b: (b, 0, 0))),
        compiler_params=_params(("arbitrary",)),
        name="kv_proj",
    )(h.reshape(batch, seq, d), g.reshape(1, d), w_k, w_vt)


def _moba_kernel(slopes_ref, q_ref, k_ref, vt_ref, km_ref, o_ref, ot_sc, t_sc, p_sc, *,
                 nb, head_dim, pairs):
    bs = MOBA_BLOCK
    pair = pl.program_id(0) % pairs
    heads = LANES // head_dim
    seq = q_ref.shape[1]
    q = q_ref[0].reshape(nb, bs, LANES)
    k = k_ref[0].reshape(nb, bs, LANES)
    km = km_ref[0].astype(BF16)
    lane = lax.broadcasted_iota(jnp.int32, (1, LANES), 1)
    pos = lax.broadcasted_iota(jnp.int32, (bs, LANES), 0).astype(F32)
    causal = (lax.broadcasted_iota(jnp.int32, (bs, bs), 1)
              >= lax.broadcasted_iota(jnp.int32, (bs, bs), 0))
    blk = lax.broadcasted_iota(jnp.int32, (nb, bs), 0)

    def split3(x):
        hi = x.astype(BF16).astype(F32)
        mid = (x - hi).astype(BF16).astype(F32)
        lo = (x - hi - mid).astype(BF16).astype(F32)
        return hi, mid, lo

    lane_b = lax.broadcasted_iota(jnp.int32, (bs, LANES), 1)

    def lanes_of(terms, base):
        out = jnp.zeros((bs, LANES), F32)
        for j, term in enumerate(terms):
            out = jnp.where(lane_b == base + j, term, out)
        return out.astype(BF16)

    slope, gate_t, q_aug, k_aug = [], [], [], []
    for hh in range(heads):
        slope.append(slopes_ref[pair * heads + hh])
        in_head = lane // head_dim == hh
        fb = ((hh + 1) % heads) * head_dim
        slope_v = jnp.full((bs, LANES), slope[hh], F32)
        q_feat = lanes_of(split3(slope_v) + split3(-slope_v * pos), fb)
        k_feat = lanes_of([pos] * 3 + [jnp.ones((bs, LANES), F32)] * 3, fb)
        q_aug.append(jnp.where(in_head, q, q_feat[None]).reshape(seq, LANES))
        k_aug.append(jnp.where(in_head, k, k_feat[None]).reshape(seq, LANES))
        gate_t.append(lax.dot_general(jnp.where(in_head, km, jnp.zeros_like(km)), q_aug[hh], NT,
                                      preferred_element_type=F32))

    def scores(i, hh, slot):
        s = lax.dot_general(k_aug[hh][0:(i + 1) * bs], q_aug[hh][i * bs:(i + 1) * bs], NT,
                            preferred_element_type=F32)
        col_max = []
        for n in range(i + 1):
            t_n = s[n * bs:(n + 1) * bs]
            if n == i:
                t_n = jnp.where(causal, t_n, NEG)
            t_sc[slot, n * bs:(n + 1) * bs, :] = t_n
            col_max.append(jnp.max(t_n, axis=0, keepdims=True))
        return col_max

    def finish(i, hh, slot, p_slot, col_max):
        if i > MOBA_TOPK:
            g = gate_t[hh][:, i * bs:(i + 1) * bs]
            rank = jnp.zeros((nb, bs), jnp.int32)
            for m in range(i):
                gm = g[m:m + 1, :]
                rank = rank + ((gm > g) | ((gm == g) & (m < blk))).astype(jnp.int32)
            sel = rank < MOBA_TOPK
        bias = []
        for n in range(i):
            b_n = -slope[hh] * float((i - n) * bs)
            if i > MOBA_TOPK:
                b_n = jnp.where(sel[n:n + 1, :], b_n, NEG)
            bias.append(b_n)
        bias.append(0.0)
        m_row = col_max[i]
        for n in range(i):
            m_row = jnp.maximum(m_row, col_max[n] + bias[n])
        l_row = jnp.zeros((1, bs), F32)
        for n in range(i + 1):
            p_n = jnp.exp2(t_sc[slot, n * bs:(n + 1) * bs, :] - (m_row - bias[n]))
            l_row = l_row + jnp.sum(p_n, axis=0, keepdims=True)
            p_sc[p_slot, n * bs:(n + 1) * bs, :] = p_n.astype(BF16)
        o_t = jnp.dot(vt_ref[0, hh * head_dim:(hh + 1) * head_dim, 0:(i + 1) * bs],
                      p_sc[p_slot, 0:(i + 1) * bs, :], preferred_element_type=F32)
        ot_sc[hh * head_dim:(hh + 1) * head_dim, i * bs:(i + 1) * bs] = o_t / l_row
        if hh == heads - 1:
            o_ref[0, i * bs:(i + 1) * bs, :] = ot_sc[:, i * bs:(i + 1) * bs].T.astype(BF16)

    items = [(i, hh) for i in reversed(range(nb)) for hh in range(heads)]
    ahead = t_sc.shape[0] - 1
    pending = [scores(*item, slot) for slot, item in enumerate(items[:ahead])]
    for idx, item in enumerate(items):
        if idx + ahead < len(items):
            pending.append(scores(*items[idx + ahead], (idx + ahead) % (ahead + 1)))
        finish(*item, idx % (ahead + 1), idx % p_sc.shape[0], pending.pop(0))


def _moba(q, k, vt, kmean, *, batch, seq, d):
    nb = seq // MOBA_BLOCK
    pairs = d // LANES
    head_dim = d // N_HEADS
    slopes = LOG2E * jnp.exp2(-8.0 * jnp.arange(1, N_HEADS + 1, dtype=F32) / N_HEADS)
    grid_spec = pltpu.PrefetchScalarGridSpec(
        num_scalar_prefetch=1,
        grid=(batch * pairs,),
        in_specs=[
            pl.BlockSpec((1, seq, LANES), lambda bp, s: (bp // pairs, 0, bp % pairs)),
            pl.BlockSpec((1, seq, LANES), lambda bp, s: (bp // pairs, 0, bp % pairs)),
            pl.BlockSpec((1, LANES, seq), lambda bp, s: (bp // pairs, bp % pairs, 0)),
            pl.BlockSpec((1, nb, LANES), lambda bp, s: (bp // pairs, 0, bp % pairs)),
        ],
        out_specs=pl.BlockSpec((1, seq, LANES), lambda bp, s: (bp // pairs, 0, bp % pairs)),
        scratch_shapes=[
            pltpu.VMEM((LANES, seq), F32),
            pltpu.VMEM((MOBA_AHEAD + 1, seq, MOBA_BLOCK), F32),
            pltpu.VMEM((2, seq, MOBA_BLOCK), BF16),
        ],
    )
    o = pl.pallas_call(
        functools.partial(_moba_kernel, nb=nb, head_dim=head_dim, pairs=pairs),
        out_shape=jax.ShapeDtypeStruct((batch, seq, d), BF16),
        grid_spec=grid_spec,
        compiler_params=_params(("arbitrary",)),
        name="moba",
    )(slopes, q.reshape(batch, seq, d), k, vt, kmean)
    return o.reshape(batch * seq, d)


def kernel(x, ffn1_norm, ffn1_w_in, ffn1_w_out, mix_norm, ffn2_norm, ffn2_w_in, ffn2_w_out,
           s5_a_re, s5_a_im, s5_b_re, s5_b_im, s5_c_re, s5_c_im, s5_d, s5_log_step, s5_w_glu,
           kv_norm, w_kv, w_q, w_o, final_norm):
    batch, seq, d = x.shape
    depth = ffn1_norm.shape[0]
    n_a = s5_a_re.shape[0]
    h = x.reshape(batch * seq, d)
    w1_in, w1_out = ffn1_w_in.astype(BF16), ffn1_w_out.astype(BF16)
    w2_in, w2_out = ffn2_w_in.astype(BF16), ffn2_w_out.astype(BF16)
    s5_params = _s5_discretize(s5_a_re, s5_a_im, s5_b_re, s5_b_im, s5_c_re, s5_c_im, s5_log_step, batch)
    w_glu = s5_w_glu.astype(BF16)
    k = vt = kmean = None
    for layer in range(depth):
        if layer == n_a:
            k, vt, kmean = _kv_proj(h, kv_norm, w_kv, batch=batch)
        if layer < n_a:
            h = _ffn(h, ffn1_norm[layer], w1_in, w1_out, layer)
            h = _s5_mixer(h, mix_norm[layer], s5_params, s5_d[layer], w_glu, layer, batch=batch)
            attn = None
        else:
            j = layer - n_a
            h, q = _ffn(h, ffn1_norm[layer], w1_in, w1_out, layer,
                        q_proj=(mix_norm[layer], w_q[j], LOG2E * (d // N_HEADS) ** -0.5))
            attn = (_moba(q, k, vt, kmean, batch=batch, seq=seq, d=d), w_o[j])
        last = layer == depth - 1
        h = _ffn(h, ffn2_norm[layer], w2_in, w2_out, layer, attn=attn, final_g=final_norm if last else None)
    return h.reshape(batch, seq, d)
```

```python
import functools

import jax
import jax.numpy as jnp
from jax import lax
from jax.experimental import pallas as pl
from jax.experimental.pallas import tpu as pltpu

S5_GROUP = 16
S5_STATE = 64
N_HEADS = 16
MOBA_BLOCK = 256
MOBA_TOPK = 3
EPS = 1e-6
NEG = -1e30
LOG2E = 1.4426950408889634
MOBA_AHEAD = 3

LANES = 128
S5_CHUNK = LANES
S5_CHUNK_STATES = S5_CHUNK // S5_GROUP * S5_STATE
VMEM_LIMIT = 56 * 1024 * 1024

BF16 = jnp.bfloat16
F32 = jnp.float32
NT = (((1,), (1,)), ((), ()))


def _rms(x, g):
    return x * lax.rsqrt(jnp.mean(x * x, axis=-1, keepdims=True) + EPS) * g


def _const_spec(shape, index=None):
    index = (0,) * len(shape) if index is None else index
    return pl.BlockSpec(shape, lambda *_: index, pipeline_mode=pl.Buffered(1))


def _params(semantics):
    return pltpu.CompilerParams(dimension_semantics=semantics, vmem_limit_bytes=VMEM_LIMIT)


def _ffn_kernel(*refs, d_ff, ck, n_w, attn_in, q_out, final_norm, q_scale):
    refs = list(refs)
    x_ref = refs.pop(0)
    attn_ref, wo_ref = (refs.pop(0), refs.pop(0)) if attn_in else (None, None)
    g_ref, win_ref, wout_ref = refs.pop(0), refs.pop(0), refs.pop(0)
    gq_ref, wq_ref = (refs.pop(0), refs.pop(0)) if q_out else (None, None)
    fg_ref = refs.pop(0) if final_norm else None
    o_ref = refs.pop(0)
    q_ref = refs.pop(0) if q_out else None
    win_sc, wout_sc = refs
    step = pl.program_id(0)

    @pl.when(step < n_w)
    def _():
        r_in, r_out = win_ref.shape[0], wout_ref.shape[0]
        win_sc[pl.ds(pl.multiple_of(step * r_in, r_in), r_in), :] = win_ref[...].astype(BF16)
        wout_sc[pl.ds(pl.multiple_of(step * r_out, r_out), r_out), :] = wout_ref[...].astype(BF16)

    @pl.when(step >= n_w)
    def _():
        x = x_ref[...]
        if attn_in:
            x = x + jnp.dot(attn_ref[...], wo_ref[...], preferred_element_type=F32)
        xn = _rms(x, g_ref[...]).astype(BF16)
        acc = jnp.zeros(x.shape, F32)
        for c in range(d_ff // ck):
            gate = jnp.dot(xn, win_sc[:, c * ck:(c + 1) * ck], preferred_element_type=F32)
            up = jnp.dot(xn, win_sc[:, d_ff + c * ck:d_ff + (c + 1) * ck], preferred_element_type=F32)
            a = (gate * jax.nn.sigmoid(gate) * up).astype(BF16)
            acc = acc + jnp.dot(a, wout_sc[c * ck:(c + 1) * ck, :], preferred_element_type=F32)
        y = x + 0.5 * acc
        if q_out:
            yn = _rms(y, gq_ref[...]).astype(BF16)
            q_ref[...] = (jnp.dot(yn, wq_ref[...], preferred_element_type=F32) * q_scale).astype(BF16)
        if final_norm:
            y = _rms(y, fg_ref[...])
        o_ref[...] = y


def _ffn(h, g, w_in, w_out, layer, *, attn=None, q_proj=None, final_g=None, tm=1024, ck=256, n_w=16):
    t, d = h.shape
    d_ff = w_out.shape[1]
    row_spec = pl.BlockSpec((tm, d), lambda i: (jnp.maximum(i - n_w, 0), 0))
    slab = lambda i: (layer, jnp.minimum(i, n_w - 1), 0)
    args, in_specs = [h], [row_spec]
    if attn is not None:
        args += [attn[0], attn[1].astype(BF16)]
        in_specs += [row_spec, _const_spec((d, d))]
    args += [g.reshape(1, d), w_in, w_out]
    in_specs += [_const_spec((1, d)), pl.BlockSpec((None, d // n_w, 2 * d_ff), slab),
                 pl.BlockSpec((None, d_ff // n_w, d), slab)]
    out_shape, out_specs = [jax.ShapeDtypeStruct((t, d), F32)], [row_spec]
    if q_proj is not None:
        args += [q_proj[0].reshape(1, d), q_proj[1].astype(BF16)]
        in_specs += [_const_spec((1, d)), _const_spec((d, d))]
        out_shape.append(jax.ShapeDtypeStruct((t, d), BF16))
        out_specs.append(row_spec)
    if final_g is not None:
        args.append(final_g.reshape(1, d))
        in_specs.append(_const_spec((1, d)))
    out = pl.pallas_call(
        functools.partial(_ffn_kernel, d_ff=d_ff, ck=ck, n_w=n_w, attn_in=attn is not None,
                          q_out=q_proj is not None, final_norm=final_g is not None,
                          q_scale=q_proj[2] if q_proj else None),
        out_shape=out_shape,
        grid=(n_w + t // tm,),
        in_specs=in_specs,
        out_specs=out_specs,
        scratch_shapes=[pltpu.VMEM((d, 2 * d_ff), BF16), pltpu.VMEM((d_ff, d), BF16)],
        compiler_params=_params(("arbitrary",)),
        name="ffn",
    )(*args)
    return out if q_proj is not None else out[0]


def _to_time_major(x, slab_sc, tl, batch):
    slabs = x.shape[1] // LANES
    for s in range(slabs):
        for b in range(batch):
            slab_sc[s, pl.ds(b, tl, stride=batch), :] = x[b * tl:(b + 1) * tl, s * LANES:(s + 1) * LANES]
    return jnp.concatenate([slab_sc[s] for s in range(slabs)], axis=-1)


def _to_batch_major(y, slab_sc, tl, batch):
    slabs = y.shape[1] // LANES
    for s in range(slabs):
        slab_sc[s] = y[:, s * LANES:(s + 1) * LANES]
    return jnp.concatenate(
        [jnp.concatenate([slab_sc[s, pl.ds(b, tl, stride=batch), :] for s in range(slabs)], axis=-1)
         for b in range(batch)], axis=0)


def _s5_kernel(h_ref, hprev_ref, g_ref, lam_ref, b2_ref, c2_ref, cb_ref, dskip_ref, wglu_ref, o_ref,
               state_sc, ynext_sc, z_sc, slab_in_sc, slab_out_sc, *, tl, n_chunks, batch):
    @pl.when(pl.program_id(0) == 0)
    def _():
        state_sc[...] = jnp.zeros(state_sc.shape, F32)
        ynext_sc[...] = jnp.zeros(ynext_sc.shape, F32)
        z_sc[...] = jnp.zeros(z_sc.shape, BF16)

    ns = S5_CHUNK_STATES
    d = h_ref.shape[-1]
    half = tl // 2
    glu_cols = 2 * d // n_chunks
    z_prev = z_sc[...]
    hn = _to_time_major(_rms(h_ref[...].reshape(batch * tl, d), g_ref[...]), slab_in_sc, tl, batch)
    hn = hn.reshape(half, 2 * batch, d)
    u_e = hn[:, :batch].reshape(half * batch, d)
    u_o = hn[:, batch:].reshape(half * batch, d)
    ue_b, uo_b = u_e.astype(BF16), u_o.astype(BF16)
    def input_proj(j):
        cols = slice(j * S5_CHUNK, (j + 1) * S5_CHUNK)
        return jnp.dot(jnp.concatenate([uo_b[:, cols], ue_b[:, cols]], axis=-1), b2_ref[j],
                       preferred_element_type=F32)

    z_e, z_o, zz = [], [], []
    bu_next = input_proj(0)
    for j in range(n_chunks):
        cols = slice(j * S5_CHUNK, (j + 1) * S5_CHUNK)
        bu = bu_next
        if j + 1 < n_chunks:
            bu_next = input_proj(j + 1)
        zz.append(jnp.dot(z_prev, wglu_ref[:, j * glu_cols:(j + 1) * glu_cols], preferred_element_type=F32))
        lam = lam_ref[j]
        lr, li = lam[:, :ns], lam[:, ns:]
        x = state_sc[j]
        xs = []
        for m in range(half):
            bu_m = bu[m * batch:(m + 1) * batch]
            xr, xi = x[:, :ns], x[:, ns:]
            nr = lr * xr - li * xi + bu_m[:, :ns]
            ni = lr * xi + li * xr + bu_m[:, ns:]
            x = jnp.concatenate([nr, ni], axis=-1)
            xs.append(x)
        state_sc[j] = x
        y2 = jnp.dot(jnp.concatenate(xs, axis=0).astype(BF16), c2_ref[j], preferred_element_type=F32)
        y_o, y_next = y2[:, :S5_CHUNK], y2[:, S5_CHUNK:]
        y_e = jnp.concatenate([ynext_sc[j], y_next[:-batch]], axis=0)
        ynext_sc[j] = y_next[-batch:]
        y_e = y_e + jnp.dot(ue_b[:, cols], cb_ref[j], preferred_element_type=F32)
        z_e.append(jax.nn.gelu(y_e + dskip_ref[:, cols] * u_e[:, cols]).astype(BF16))
        z_o.append(jax.nn.gelu(y_o + dskip_ref[:, cols] * u_o[:, cols]).astype(BF16))
    z_sc[...] = jnp.concatenate([jnp.concatenate(z_e, axis=-1), jnp.concatenate(z_o, axis=-1)], axis=0)
    zz = jnp.concatenate(zz, axis=-1)
    mixed = zz[:, :d] * jax.nn.sigmoid(zz[:, d:])
    mixed = jnp.concatenate([mixed[:half * batch].reshape(half, batch, d),
                             mixed[half * batch:].reshape(half, batch, d)], axis=1).reshape(tl * batch, d)
    mixed = _to_batch_major(mixed, slab_out_sc, tl, batch)
    o_ref[...] = hprev_ref[...] + mixed.reshape(batch, tl, d)


def _s5_discretize(a_re, a_im, b_re, b_im, c_re, c_im, log_step, batch):
    n, g, p = a_re.shape
    gc = S5_CHUNK // S5_GROUP
    n_chunks = g // gc
    dt = jnp.exp(log_step)[..., None]
    ar = jnp.minimum(a_re, -1e-4)
    ai = a_im
    mag = jnp.exp(ar * dt)
    lr = mag * jnp.cos(ai * dt)
    li = mag * jnp.sin(ai * dt)
    nr = lr - 1.0
    den = ar * ar + ai * ai
    fr = (nr * ar + li * ai) / den
    fi = (li * ar - nr * ai) / den
    bbr = fr[..., None] * b_re - fi[..., None] * b_im
    bbi = fr[..., None] * b_im + fi[..., None] * b_re
    lbr = lr[..., None] * bbr - li[..., None] * bbi
    lbi = lr[..., None] * bbi + li[..., None] * bbr
    clr = c_re * lr[:, :, None, :] - c_im * li[:, :, None, :]
    cli = c_re * li[:, :, None, :] + c_im * lr[:, :, None, :]
    cb = jnp.einsum('ngjp,ngpk->ngkj', c_re, bbr) - jnp.einsum('ngjp,ngpk->ngkj', c_im, bbi)
    own = jnp.repeat(jnp.eye(gc, dtype=F32), p, axis=1)[:, None, :]
    own_out = jnp.repeat(jnp.eye(gc, dtype=F32), S5_GROUP, axis=1)[:, None, :]

    def bd_in(bb):
        bb = jnp.swapaxes(bb.reshape(n, n_chunks, gc, p, S5_GROUP), -1, -2)
        return (jnp.tile(bb, (1, 1, 1, 1, gc)) * own).reshape(n, n_chunks, S5_CHUNK, gc * p)

    def bd_out(cc):
        cc = jnp.swapaxes(cc.reshape(n, n_chunks, gc, S5_GROUP, p), -1, -2)
        return (jnp.tile(cc, (1, 1, 1, 1, gc)) * own_out).reshape(n, n_chunks, gc * p, S5_CHUNK)

    b2 = jnp.concatenate([jnp.concatenate([bd_in(bbr), bd_in(bbi)], axis=-1),
                          jnp.concatenate([bd_in(lbr), bd_in(lbi)], axis=-1)], axis=2)
    c2 = jnp.concatenate([jnp.concatenate([bd_out(c_re), bd_out(-c_im)], axis=2),
                          jnp.concatenate([bd_out(clr), bd_out(-cli)], axis=2)], axis=-1)
    cb = (jnp.tile(cb.reshape(n, n_chunks, gc, S5_GROUP, S5_GROUP), (1, 1, 1, 1, gc)) * own_out)
    cb = cb.reshape(n, n_chunks, S5_CHUNK, S5_CHUNK)
    lam2 = jnp.concatenate([(lr * lr - li * li).reshape(n, n_chunks, 1, gc * p),
                            (2.0 * lr * li).reshape(n, n_chunks, 1, gc * p)], axis=-1)
    lam2 = jnp.broadcast_to(lam2, (n, n_chunks, batch, 2 * gc * p))
    return lam2, b2.astype(BF16), c2.astype(BF16), cb.astype(BF16)


def _s5_mixer(h, g, s5_params, d_skip, w_glu, layer, *, batch, tl=64):
    t, d = h.shape
    seq = t // batch
    n_chunks = d // S5_CHUNK
    ns2 = 2 * S5_CHUNK_STATES
    rows = tl * batch
    nt = seq // tl
    h3 = h.reshape(batch, seq, d)
    out = pl.pallas_call(
        functools.partial(_s5_kernel, tl=tl, n_chunks=n_chunks, batch=batch),
        out_shape=jax.ShapeDtypeStruct((batch, seq, d), F32),
        grid=(nt + 1,),
        in_specs=[
            pl.BlockSpec((batch, tl, d), lambda i: (0, jnp.minimum(i, nt - 1), 0)),
            pl.BlockSpec((batch, tl, d), lambda i: (0, jnp.maximum(i - 1, 0), 0)),
            _const_spec((1, d)),
            _const_spec((None, n_chunks, batch, ns2), (layer, 0, 0, 0)),
            _const_spec((None, n_chunks, 2 * S5_CHUNK, ns2), (layer, 0, 0, 0)),
            _const_spec((None, n_chunks, ns2, 2 * S5_CHUNK), (layer, 0, 0, 0)),
            _const_spec((None, n_chunks, S5_CHUNK, S5_CHUNK), (layer, 0, 0, 0)),
            _const_spec((1, d)),
            _const_spec((None, d, 2 * d), (layer, 0, 0)),
        ],
        out_specs=pl.BlockSpec((batch, tl, d), lambda i: (0, jnp.maximum(i - 1, 0), 0)),
        scratch_shapes=[
            pltpu.VMEM((n_chunks, batch, ns2), F32),
            pltpu.VMEM((n_chunks, batch, S5_CHUNK), F32),
            pltpu.VMEM((rows, d), BF16),
            pltpu.VMEM((d // LANES, rows, LANES), F32),
            pltpu.VMEM((d // LANES, rows, LANES), F32),
        ],
        compiler_params=_params(("arbitrary",)),
        name="s5_mixer",
    )(h3, h3, g.reshape(1, d), *s5_params, d_skip.reshape(1, d), w_glu)
    return out.reshape(t, d)


def _kv_kernel(x_ref, g_ref, wk_ref, wvt_ref, k_ref, vt_ref, km_ref, *, ck):
    xn = _rms(x_ref[0], g_ref[...]).astype(BF16)
    seq, d = xn.shape
    nb = seq // MOBA_BLOCK
    for c in range(d // ck):
        cols = slice(c * ck, (c + 1) * ck)
        kc = jnp.dot(xn, wk_ref[:, cols], preferred_element_type=F32)
        k_ref[0, :, cols] = kc.astype(BF16)
        km_ref[0, :, cols] = jnp.mean(kc.reshape(nb, MOBA_BLOCK, ck), axis=1)
        vt_ref[0, cols, :] = lax.dot_general(wvt_ref[cols, :], xn, NT,
                                             preferred_element_type=F32).astype(BF16)


def _kv_proj(h, g, w_kv, *, batch, ck=256):
    t, d = h.shape
    seq = t // batch
    nb = seq // MOBA_BLOCK
    w_k = w_kv[:, :d].astype(BF16)
    w_vt = w_kv[:, d:].T.astype(BF16)
    return pl.pallas_call(
        functools.partial(_kv_kernel, ck=ck),
        out_shape=(jax.ShapeDtypeStruct((batch, seq, d), BF16),
                   jax.ShapeDtypeStruct((batch, d, seq), BF16),
                   jax.ShapeDtypeStruct((batch, nb, d), F32)),
        grid=(batch,),
        in_specs=[
            pl.BlockSpec((1, seq, d), lambda b: (b, 0, 0)),
            _const_spec((1, d)),
            _const_spec((d, d)),
            _const_spec((d, d)),
        ],
        out_specs=(pl.BlockSpec((1, seq, d), lambda b: (b, 0, 0)),
                   pl.BlockSpec((1, d, seq), lambda b: (b, 0, 0)),
                   pl.BlockSpec((1, nb, d), lambda b: (b, 0, 0))),
        compiler_params=_params(("arbitrary",)),
        name="kv_proj",
    )(h.reshape(batch, seq, d), g.reshape(1, d), w_k, w_vt)


def _moba_kernel(slopes_ref, q_ref, k_ref, vt_ref, km_ref, o_ref, ot_sc, t_sc, p_sc, *,
                 nb, head_dim, pairs):
    bs = MOBA_BLOCK
    pair = pl.program_id(0) % pairs
    heads = LANES // head_dim
    seq = q_ref.shape[1]
    q = q_ref[0].reshape(nb, bs, LANES)
    k = k_ref[0].reshape(nb, bs, LANES)
    km = km_ref[0].astype(BF16)
    lane = lax.broadcasted_iota(jnp.int32, (1, LANES), 1)
    pos = lax.broadcasted_iota(jnp.int32, (bs, LANES), 0).astype(F32)
    causal = (lax.broadcasted_iota(jnp.int32, (bs, bs), 1)
              >= lax.broadcasted_iota(jnp.int32, (bs, bs), 0))
    blk = lax.broadcasted_iota(jnp.int32, (nb, bs), 0)

    def split3(x):
        hi = x.astype(BF16).astype(F32)
        mid = (x - hi).astype(BF16).astype(F32)
        lo = (x - hi - mid).astype(BF16).astype(F32)
        return hi, mid, lo

    lane_b = lax.broadcasted_iota(jnp.int32, (bs, LANES), 1)

    def lanes_of(terms, base):
        out = jnp.zeros((bs, LANES), F32)
        for j, term in enumerate(terms):
            out = jnp.where(lane_b == base + j, term, out)
        return out.astype(BF16)

    slope, gate_t, q_aug, k_aug = [], [], [], []
    for hh in range(heads):
        slope.append(slopes_ref[pair * heads + hh])
        in_head = lane // head_dim == hh
        fb = ((hh + 1) % heads) * head_dim
        slope_v = jnp.full((bs, LANES), slope[hh], F32)
        q_feat = lanes_of(split3(slope_v) + split3(-slope_v * pos), fb)
        k_feat = lanes_of([pos] * 3 + [jnp.ones((bs, LANES), F32)] * 3, fb)
        q_aug.append(jnp.where(in_head, q, q_feat[None]).reshape(seq, LANES))
        k_aug.append(jnp.where(in_head, k, k_feat[None]).reshape(seq, LANES))
        gate_t.append(lax.dot_general(jnp.where(in_head, km, jnp.zeros_like(km)), q_aug[hh], NT,
                                      preferred_element_type=F32))

    def scores(i, hh, slot):
        s = lax.dot_general(k_aug[hh][0:(i + 1) * bs], q_aug[hh][i * bs:(i + 1) * bs], NT,
                            preferred_element_type=F32)
        col_max = []
        for n in range(i + 1):
            t_n = s[n * bs:(n + 1) * bs]
            if n == i:
                t_n = jnp.where(causal, t_n, NEG)
            t_sc[slot, n * bs:(n + 1) * bs, :] = t_n
            col_max.append(jnp.max(t_n, axis=0, keepdims=True))
        return col_max

    def finish(i, hh, slot, p_slot, col_max):
        if i > MOBA_TOPK:
            g = gate_t[hh][:, i * bs:(i + 1) * bs]
            rank = jnp.zeros((nb, bs), jnp.int32)
            for m in range(i):
                gm = g[m:m + 1, :]
                rank = rank + ((gm > g) | ((gm == g) & (m < blk))).astype(jnp.int32)
            sel = rank < MOBA_TOPK
        bias = []
        for n in range(i):
            b_n = -slope[hh] * float((i - n) * bs)
            if i > MOBA_TOPK:
                b_n = jnp.where(sel[n:n + 1, :], b_n, NEG)
            bias.append(b_n)
        bias.append(0.0)
        m_row = col_max[i]
        for n in range(i):
            m_row = jnp.maximum(m_row, col_max[n] + bias[n])
        l_row = jnp.zeros((1, bs), F32)
        for n in range(i + 1):
            p_n = jnp.exp2(t_sc[slot, n * bs:(n + 1) * bs, :] - (m_row - bias[n]))
            l_row = l_row + jnp.sum(p_n, axis=0, keepdims=True)
            p_sc[p_slot, n * bs:(n + 1) * bs, :] = p_n.astype(BF16)
        o_t = jnp.dot(vt_ref[0, hh * head_dim:(hh + 1) * head_dim, 0:(i + 1) * bs],
                      p_sc[p_slot, 0:(i + 1) * bs, :], preferred_element_type=F32)
        ot_sc[hh * head_dim:(hh + 1) * head_dim, i * bs:(i + 1) * bs] = o_t / l_row
        if hh == heads - 1:
            o_ref[0, i * bs:(i + 1) * bs, :] = ot_sc[:, i * bs:(i + 1) * bs].T.astype(BF16)

    items = [(i, hh) for i in reversed(range(nb)) for hh in range(heads)]
    ahead = t_sc.shape[0] - 1
    pending = [scores(*item, slot) for slot, item in enumerate(items[:ahead])]
    for idx, item in enumerate(items):
        if idx + ahead < len(items):
            pending.append(scores(*items[idx + ahead], (idx + ahead) % (ahead + 1)))
        finish(*item, idx % (ahead + 1), idx % p_sc.shape[0], pending.pop(0))


def _moba(q, k, vt, kmean, *, batch, seq, d):
    nb = seq // MOBA_BLOCK
    pairs = d // LANES
    head_dim = d // N_HEADS
    slopes = LOG2E * jnp.exp2(-8.0 * jnp.arange(1, N_HEADS + 1, dtype=F32) / N_HEADS)
    grid_spec = pltpu.PrefetchScalarGridSpec(
        num_scalar_prefetch=1,
        grid=(batch * pairs,),
        in_specs=[
            pl.BlockSpec((1, seq, LANES), lambda bp, s: (bp // pairs, 0, bp % pairs)),
            pl.BlockSpec((1, seq, LANES), lambda bp, s: (bp // pairs, 0, bp % pairs)),
            pl.BlockSpec((1, LANES, seq), lambda bp, s: (bp // pairs, bp % pairs, 0)),
            pl.BlockSpec((1, nb, LANES), lambda bp, s: (bp // pairs, 0, bp % pairs)),
        ],
        out_specs=pl.BlockSpec((1, seq, LANES), lambda bp, s: (bp // pairs, 0, bp % pairs)),
        scratch_shapes=[
            pltpu.VMEM((LANES, seq), F32),
            pltpu.VMEM((MOBA_AHEAD + 1, seq, MOBA_BLOCK), F32),
            pltpu.VMEM((2, seq, MOBA_BLOCK), BF16),
        ],
    )
    o = pl.pallas_call(
        functools.partial(_moba_kernel, nb=nb, head_dim=head_dim, pairs=pairs),
        out_shape=jax.ShapeDtypeStruct((batch, seq, d), BF16),
        grid_spec=grid_spec,
        compiler_params=_params(("arbitrary",)),
        name="moba",
    )(slopes, q.reshape(batch, seq, d), k, vt, kmean)
    return o.reshape(batch * seq, d)


def kernel(x, ffn1_norm, ffn1_w_in, ffn1_w_out, mix_norm, ffn2_norm, ffn2_w_in, ffn2_w_out,
           s5_a_re, s5_a_im, s5_b_re, s5_b_im, s5_c_re, s5_c_im, s5_d, s5_log_step, s5_w_glu,
           kv_norm, w_kv, w_q, w_o, final_norm):
    batch, seq, d = x.shape
    depth = ffn1_norm.shape[0]
    n_a = s5_a_re.shape[0]
    h = x.reshape(batch * seq, d)
    s5_params = _s5_discretize(s5_a_re, s5_a_im, s5_b_re, s5_b_im, s5_c_re, s5_c_im, s5_log_step, batch)
    w_glu = s5_w_glu.astype(BF16)
    k = vt = kmean = None
    for layer in range(depth):
        if layer == n_a:
            k, vt, kmean = _kv_proj(h, kv_norm, w_kv, batch=batch)
        if layer < n_a:
            h = _ffn(h, ffn1_norm[layer], ffn1_w_in, ffn1_w_out, layer)
            h = _s5_mixer(h, mix_norm[layer], s5_params, s5_d[layer], w_glu, layer, batch=batch)
            attn = None
        else:
            j = layer - n_a
            h, q = _ffn(h, ffn1_norm[layer], ffn1_w_in, ffn1_w_out, layer,
                        q_proj=(mix_norm[layer], w_q[j], LOG2E * (d // N_HEADS) ** -0.5))
            attn = (_moba(q, k, vt, kmean, batch=batch, seq=seq, d=d), w_o[j])
        last = layer == depth - 1
        h = _ffn(h, ffn2_norm[layer], ffn2_w_in, ffn2_w_out, layer, attn=attn, final_g=final_norm if last else None)
    return h.reshape(batch, seq, d)
```

```python
import functools

import jax
import jax.numpy as jnp
from jax import lax
from jax.experimental import pallas as pl
from jax.experimental.pallas import tpu as pltpu

S5_GROUP = 16
S5_STATE = 64
N_HEADS = 16
MOBA_BLOCK = 256
MOBA_TOPK = 3
EPS = 1e-6
NEG = -1e30
LOG2E = 1.4426950408889634
MOBA_AHEAD = 3

LANES = 128
S5_CHUNK = LANES
S5_CHUNK_STATES = S5_CHUNK // S5_GROUP * S5_STATE
VMEM_LIMIT = 56 * 1024 * 1024

BF16 = jnp.bfloat16
F32 = jnp.float32
NT = (((1,), (1,)), ((), ()))


def _rms(x, g):
    return x * lax.rsqrt(jnp.mean(x * x, axis=-1, keepdims=True) + EPS) * g


def _const_spec(shape, index=None):
    index = (0,) * len(shape) if index is None else index
    return pl.BlockSpec(shape, lambda *_: index, pipeline_mode=pl.Buffered(1))


def _params(semantics):
    return pltpu.CompilerParams(dimension_semantics=semantics, vmem_limit_bytes=VMEM_LIMIT)


def _ffn_kernel(*refs, d_ff, ck, n_w, attn_in, q_out, final_norm, q_scale):
    refs = list(refs)
    x_ref = refs.pop(0)
    attn_ref, wo_ref = (refs.pop(0), refs.pop(0)) if attn_in else (None, None)
    g_ref, win_ref, wout_ref = refs.pop(0), refs.pop(0), refs.pop(0)
    gq_ref, wq_ref = (refs.pop(0), refs.pop(0)) if q_out else (None, None)
    fg_ref = refs.pop(0) if final_norm else None
    o_ref = refs.pop(0)
    q_ref = refs.pop(0) if q_out else None
    win_sc, wout_sc = refs
    step = pl.program_id(0)

    @pl.when(step < n_w)
    def _():
        r_in, r_out = win_ref.shape[0], wout_ref.shape[0]
        win_sc[pl.ds(pl.multiple_of(step * r_in, r_in), r_in), :] = win_ref[...].astype(BF16)
        wout_sc[pl.ds(pl.multiple_of(step * r_out, r_out), r_out), :] = wout_ref[...].astype(BF16)

    @pl.when(step >= n_w)
    def _():
        x = x_ref[...]
        if attn_in:
            x = x + jnp.dot(attn_ref[...], wo_ref[...], preferred_element_type=F32)
        xn = _rms(x, g_ref[...]).astype(BF16)
        acc = jnp.zeros(x.shape, F32)
        for c in range(d_ff // ck):
            gate = jnp.dot(xn, win_sc[:, c * ck:(c + 1) * ck], preferred_element_type=F32)
            up = jnp.dot(xn, win_sc[:, d_ff + c * ck:d_ff + (c + 1) * ck], preferred_element_type=F32)
            a = (gate * jax.nn.sigmoid(gate) * up).astype(BF16)
            acc = acc + jnp.dot(a, wout_sc[c * ck:(c + 1) * ck, :], preferred_element_type=F32)
        y = x + 0.5 * acc
        if q_out:
            yn = _rms(y, gq_ref[...]).astype(BF16)
            q_ref[...] = (jnp.dot(yn, wq_ref[...], preferred_element_type=F32) * q_scale).astype(BF16)
        if final_norm:
            y = _rms(y, fg_ref[...])
        o_ref[...] = y


def _ffn(h, g, w_in, w_out, layer, *, attn=None, q_proj=None, final_g=None, tm=1024, ck=256, n_w=16):
    t, d = h.shape
    d_ff = w_out.shape[1]
    row_spec = pl.BlockSpec((tm, d), lambda i: (jnp.maximum(i - n_w, 0), 0))
    slab = lambda i: (layer, jnp.minimum(i, n_w - 1), 0)
    args, in_specs = [h], [row_spec]
    if attn is not None:
        args += [attn[0], attn[1].astype(BF16)]
        in_specs += [row_spec, _const_spec((d, d))]
    args += [g.reshape(1, d), w_in, w_out]
    in_specs += [_const_spec((1, d)), pl.BlockSpec((None, d // n_w, 2 * d_ff), slab),
                 pl.BlockSpec((None, d_ff // n_w, d), slab)]
    out_shape, out_specs = [jax.ShapeDtypeStruct((t, d), F32)], [row_spec]
    if q_proj is not None:
        args += [q_proj[0].reshape(1, d), q_proj[1].astype(BF16)]
        in_specs += [_const_spec((1, d)), _const_spec((d, d))]
        out_shape.append(jax.ShapeDtypeStruct((t, d), BF16))
        out_specs.append(row_spec)
    if final_g is not None:
        args.append(final_g.reshape(1, d))
        in_specs.append(_const_spec((1, d)))
    out = pl.pallas_call(
        functools.partial(_ffn_kernel, d_ff=d_ff, ck=ck, n_w=n_w, attn_in=attn is not None,
                          q_out=q_proj is not None, final_norm=final_g is not None,
                          q_scale=q_proj[2] if q_proj else None),
        out_shape=out_shape,
        grid=(n_w + t // tm,),
        in_specs=in_specs,
        out_specs=out_specs,
        scratch_shapes=[pltpu.VMEM((d, 2 * d_ff), BF16), pltpu.VMEM((d_ff, d), BF16)],
        compiler_params=_params(("arbitrary",)),
        name="ffn",
    )(*args)
    return out if q_proj is not None else out[0]


def _to_time_major(x, slab_sc, tl, batch):
    slabs = x.shape[1] // LANES
    for s in range(slabs):
        for b in range(batch):
            slab_sc[s, pl.ds(b, tl, stride=batch), :] = x[b * tl:(b + 1) * tl, s * LANES:(s + 1) * LANES]
    return jnp.concatenate([slab_sc[s] for s in range(slabs)], axis=-1)


def _to_batch_major(y, slab_sc, tl, batch):
    slabs = y.shape[1] // LANES
    for s in range(slabs):
        slab_sc[s] = y[:, s * LANES:(s + 1) * LANES]
    return jnp.concatenate(
        [jnp.concatenate([slab_sc[s, pl.ds(b, tl, stride=batch), :] for s in range(slabs)], axis=-1)
         for b in range(batch)], axis=0)


def _s5_kernel(h_ref, hprev_ref, g_ref, lam_ref, b2_ref, c2_ref, cb_ref, dskip_ref, wglu_ref, o_ref,
               state_sc, ynext_sc, z_sc, slab_in_sc, slab_out_sc, *, tl, n_chunks, batch):
    @pl.when(pl.program_id(0) == 0)
    def _():
        state_sc[...] = jnp.zeros(state_sc.shape, F32)
        ynext_sc[...] = jnp.zeros(ynext_sc.shape, F32)
        z_sc[...] = jnp.zeros(z_sc.shape, BF16)

    ns = S5_CHUNK_STATES
    d = h_ref.shape[-1]
    half = tl // 2
    glu_cols = 2 * d // n_chunks
    z_prev = z_sc[...]
    hn = _to_time_major(_rms(h_ref[...].reshape(batch * tl, d), g_ref[...]), slab_in_sc, tl, batch)
    hn = hn.reshape(half, 2 * batch, d)
    u_e = hn[:, :batch].reshape(half * batch, d)
    u_o = hn[:, batch:].reshape(half * batch, d)
    ue_b, uo_b = u_e.astype(BF16), u_o.astype(BF16)
    def input_proj(j):
        cols = slice(j * S5_CHUNK, (j + 1) * S5_CHUNK)
        return jnp.dot(jnp.concatenate([uo_b[:, cols], ue_b[:, cols]], axis=-1), b2_ref[j],
                       preferred_element_type=F32)

    z_e, z_o, zz = [], [], []
    bu_next = input_proj(0)
    for j in range(n_chunks):
        cols = slice(j * S5_CHUNK, (j + 1) * S5_CHUNK)
        bu = bu_next
        if j + 1 < n_chunks:
            bu_next = input_proj(j + 1)
        zz.append(jnp.dot(z_prev, wglu_ref[:, j * glu_cols:(j + 1) * glu_cols], preferred_element_type=F32))
        lam = lam_ref[j]
        lr, li = lam[:, :ns], lam[:, ns:]
        x = state_sc[j]
        xs = []
        for m in range(half):
            bu_m = bu[m * batch:(m + 1) * batch]
            xr, xi = x[:, :ns], x[:, ns:]
            nr = lr * xr - li * xi + bu_m[:, :ns]
            ni = lr * xi + li * xr + bu_m[:, ns:]
            x = jnp.concatenate([nr, ni], axis=-1)
            xs.append(x)
        state_sc[j] = x
        y2 = jnp.dot(jnp.concatenate(xs, axis=0).astype(BF16), c2_ref[j], preferred_element_type=F32)
        y_o, y_next = y2[:, :S5_CHUNK], y2[:, S5_CHUNK:]
        y_e = jnp.concatenate([ynext_sc[j], y_next[:-batch]], axis=0)
        ynext_sc[j] = y_next[-batch:]
        y_e = y_e + jnp.dot(ue_b[:, cols], cb_ref[j], preferred_element_type=F32)
        z_e.append(jax.nn.gelu(y_e + dskip_ref[:, cols] * u_e[:, cols]).astype(BF16))
        z_o.append(jax.nn.gelu(y_o + dskip_ref[:, cols] * u_o[:, cols]).astype(BF16))
    z_sc[...] = jnp.concatenate([jnp.concatenate(z_e, axis=-1), jnp.concatenate(z_o, axis=-1)], axis=0)
    zz = jnp.concatenate(zz, axis=-1)
    mixed = zz[:, :d] * jax.nn.sigmoid(zz[:, d:])
    mixed = jnp.concatenate([mixed[:half * batch].reshape(half, batch, d),
                             mixed[half * batch:].reshape(half, batch, d)], axis=1).reshape(tl * batch, d)
    mixed = _to_batch_major(mixed, slab_out_sc, tl, batch)
    o_ref[...] = hprev_ref[...] + mixed.reshape(batch, tl, d)


def _s5_discretize(a_re, a_im, b_re, b_im, c_re, c_im, log_step, batch):
    n, g, p = a_re.shape
    gc = S5_CHUNK // S5_GROUP
    n_chunks = g // gc
    dt = jnp.exp(log_step)[..., None]
    ar = jnp.minimum(a_re, -1e-4)
    ai = a_im
    mag = jnp.exp(ar * dt)
    lr = mag * jnp.cos(ai * dt)
    li = mag * jnp.sin(ai * dt)
    nr = lr - 1.0
    den = ar * ar + ai * ai
    fr = (nr * ar + li * ai) / den
    fi = (li * ar - nr * ai) / den
    bbr = fr[..., None] * b_re - fi[..., None] * b_im
    bbi = fr[..., None] * b_im + fi[..., None] * b_re
    lbr = lr[..., None] * bbr - li[..., None] * bbi
    lbi = lr[..., None] * bbi + li[..., None] * bbr
    clr = c_re * lr[:, :, None, :] - c_im * li[:, :, None, :]
    cli = c_re * li[:, :, None, :] + c_im * lr[:, :, None, :]
    cb = jnp.einsum('ngjp,ngpk->ngkj', c_re, bbr) - jnp.einsum('ngjp,ngpk->ngkj', c_im, bbi)
    eye = jnp.eye(gc, dtype=F32)
    xb = jnp.stack([jnp.stack([bbr, bbi]), jnp.stack([lbr, lbi])]).reshape(2, 2, n, n_chunks, gc, p, S5_GROUP)
    b2 = jnp.einsum('rqncgpk,gh->ncrgkqhp', xb, eye).reshape(n, n_chunks, 2 * S5_CHUNK, 2 * gc * p)
    xc = jnp.stack([jnp.stack([c_re, clr]), jnp.stack([-c_im, -cli])]).reshape(2, 2, n, n_chunks, gc, S5_GROUP, p)
    c2 = jnp.einsum('qsncgjp,gh->ncqgpshj', xc, eye).reshape(n, n_chunks, 2 * gc * p, 2 * S5_CHUNK)
    cb = jnp.einsum('ncgkj,gh->ncgkhj', cb.reshape(n, n_chunks, gc, S5_GROUP, S5_GROUP), eye)
    cb = cb.reshape(n, n_chunks, S5_CHUNK, S5_CHUNK)
    lam2 = jnp.concatenate([(lr * lr - li * li).reshape(n, n_chunks, 1, gc * p),
                            (2.0 * lr * li).reshape(n, n_chunks, 1, gc * p)], axis=-1)
    lam2 = jnp.broadcast_to(lam2, (n, n_chunks, batch, 2 * gc * p))
    return lam2, b2.astype(BF16), c2.astype(BF16), cb.astype(BF16)


def _s5_mixer(h, g, s5_params, d_skip, w_glu, layer, *, batch, tl=64):
    t, d = h.shape
    seq = t // batch
    n_chunks = d // S5_CHUNK
    ns2 = 2 * S5_CHUNK_STATES
    rows = tl * batch
    nt = seq // tl
    h3 = h.reshape(batch, seq, d)
    out = pl.pallas_call(
        functools.partial(_s5_kernel, tl=tl, n_chunks=n_chunks, batch=batch),
        out_shape=jax.ShapeDtypeStruct((batch, seq, d), F32),
        grid=(nt + 1,),
        in_specs=[
            pl.BlockSpec((batch, tl, d), lambda i: (0, jnp.minimum(i, nt - 1), 0)),
            pl.BlockSpec((batch, tl, d), lambda i: (0, jnp.maximum(i - 1, 0), 0)),
            _const_spec((1, d)),
            _const_spec((None, n_chunks, batch, ns2), (layer, 0, 0, 0)),
            _const_spec((None, n_chunks, 2 * S5_CHUNK, ns2), (layer, 0, 0, 0)),
            _const_spec((None, n_chunks, ns2, 2 * S5_CHUNK), (layer, 0, 0, 0)),
            _const_spec((None, n_chunks, S5_CHUNK, S5_CHUNK), (layer, 0, 0, 0)),
            _const_spec((1, d)),
            _const_spec((None, d, 2 * d), (layer, 0, 0)),
        ],
        out_specs=pl.BlockSpec((batch, tl, d), lambda i: (0, jnp.maximum(i - 1, 0), 0)),
        scratch_shapes=[
            pltpu.VMEM((n_chunks, batch, ns2), F32),
            pltpu.VMEM((n_chunks, batch, S5_CHUNK), F32),
            pltpu.VMEM((rows, d), BF16),
            pltpu.VMEM((d // LANES, rows, LANES), F32),
            pltpu.VMEM((d // LANES, rows, LANES), F32),
        ],
        compiler_params=_params(("arbitrary",)),
        name="s5_mixer",
    )(h3, h3, g.reshape(1, d), *s5_params, d_skip.reshape(1, d), w_glu)
    return out.reshape(t, d)


def _kv_kernel(x_ref, g_ref, wk_ref, wvt_ref, k_ref, vt_ref, km_ref, *, ck):
    xn = _rms(x_ref[0], g_ref[...]).astype(BF16)
    seq, d = xn.shape
    nb = seq // MOBA_BLOCK
    for c in range(d // ck):
        cols = slice(c * ck, (c + 1) * ck)
        kc = jnp.dot(xn, wk_ref[:, cols], preferred_element_type=F32)
        k_ref[0, :, cols] = kc.astype(BF16)
        km_ref[0, :, cols] = jnp.mean(kc.reshape(nb, MOBA_BLOCK, ck), axis=1)
        vt_ref[0, cols, :] = lax.dot_general(wvt_ref[cols, :], xn, NT,
                                             preferred_element_type=F32).astype(BF16)


def _kv_proj(h, g, w_kv, *, batch, ck=256):
    t, d = h.shape
    seq = t // batch
    nb = seq // MOBA_BLOCK
    w_k = w_kv[:, :d].astype(BF16)
    w_vt = w_kv[:, d:].T.astype(BF16)
    return pl.pallas_call(
        functools.partial(_kv_kernel, ck=ck),
        out_shape=(jax.ShapeDtypeStruct((batch, seq, d), BF16),
                   jax.ShapeDtypeStruct((batch, d, seq), BF16),
                   jax.ShapeDtypeStruct((batch, nb, d), F32)),
        grid=(batch,),
        in_specs=[
            pl.BlockSpec((1, seq, d), lambda b: (b, 0, 0)),
            _const_spec((1, d)),
            _const_spec((d, d)),
            _const_spec((d, d)),
        ],
        out_specs=(pl.BlockSpec((1, seq, d), lambda b: (b, 0, 0)),
                   pl.BlockSpec((1, d, seq), lambda b: (b, 0, 0)),
                   pl.BlockSpec((1, nb, d), lambda b: (b, 0, 0))),
        compiler_params=_params(("arbitrary",)),
        name="kv_proj",
    )(h.reshape(batch, seq, d), g.reshape(1, d), w_k, w_vt)


def _moba_kernel(slopes_ref, q_ref, k_ref, vt_ref, km_ref, o_ref, ot_sc, t_sc, p_sc, *,
                 nb, head_dim, pairs):
    bs = MOBA_BLOCK
    pair = pl.program_id(0) % pairs
    heads = LANES // head_dim
    seq = q_ref.shape[1]
    q = q_ref[0].reshape(nb, bs, LANES)
    k = k_ref[0].reshape(nb, bs, LANES)
    km = km_ref[0].astype(BF16)
    lane = lax.broadcasted_iota(jnp.int32, (1, LANES), 1)
    pos = lax.broadcasted_iota(jnp.int32, (bs, LANES), 0).astype(F32)
    causal = (lax.broadcasted_iota(jnp.int32, (bs, bs), 1)
              >= lax.broadcasted_iota(jnp.int32, (bs, bs), 0))
    blk = lax.broadcasted_iota(jnp.int32, (nb, bs), 0)

    def split3(x):
        hi = x.astype(BF16).astype(F32)
        mid = (x - hi).astype(BF16).astype(F32)
        lo = (x - hi - mid).astype(BF16).astype(F32)
        return hi, mid, lo

    lane_b = lax.broadcasted_iota(jnp.int32, (bs, LANES), 1)

    def lanes_of(terms, base):
        out = jnp.zeros((bs, LANES), F32)
        for j, term in enumerate(terms):
            out = jnp.where(lane_b == base + j, term, out)
        return out.astype(BF16)

    slope, gate_t, q_aug, k_aug = [], [], [], []
    for hh in range(heads):
        slope.append(slopes_ref[pair * heads + hh])
        in_head = lane // head_dim == hh
        fb = ((hh + 1) % heads) * head_dim
        slope_v = jnp.full((bs, LANES), slope[hh], F32)
        q_feat = lanes_of(split3(slope_v) + split3(-slope_v * pos), fb)
        k_feat = lanes_of([pos] * 3 + [jnp.ones((bs, LANES), F32)] * 3, fb)
        q_aug.append(jnp.where(in_head, q, q_feat[None]).reshape(seq, LANES))
        k_aug.append(jnp.where(in_head, k, k_feat[None]).reshape(seq, LANES))
        gate_t.append(lax.dot_general(jnp.where(in_head, km, jnp.zeros_like(km)), q_aug[hh], NT,
                                      preferred_element_type=F32))

    def scores(i, hh, slot):
        s = lax.dot_general(k_aug[hh][0:(i + 1) * bs], q_aug[hh][i * bs:(i + 1) * bs], NT,
                            preferred_element_type=F32)
        col_max = []
        for n in range(i + 1):
            t_n = s[n * bs:(n + 1) * bs]
            if n == i:
                t_n = jnp.where(causal, t_n, NEG)
            t_sc[slot, n * bs:(n + 1) * bs, :] = t_n
            col_max.append(jnp.max(t_n, axis=0, keepdims=True))
        return col_max

    def finish(i, hh, slot, p_slot, col_max):
        if i > MOBA_TOPK:
            g = gate_t[hh][:, i * bs:(i + 1) * bs]
            rank = jnp.zeros((nb, bs), jnp.int32)
            for m in range(i):
                gm = g[m:m + 1, :]
                rank = rank + ((gm > g) | ((gm == g) & (m < blk))).astype(jnp.int32)
            sel = rank < MOBA_TOPK
        bias = []
        for n in range(i):
            b_n = -slope[hh] * float((i - n) * bs)
            if i > MOBA_TOPK:
                b_n = jnp.where(sel[n:n + 1, :], b_n, NEG)
            bias.append(b_n)
        bias.append(0.0)
        m_row = col_max[i]
        for n in range(i):
            m_row = jnp.maximum(m_row, col_max[n] + bias[n])
        l_row = jnp.zeros((1, bs), F32)
        for n in range(i + 1):
            p_n = jnp.exp2(t_sc[slot, n * bs:(n + 1) * bs, :] - (m_row - bias[n]))
            l_row = l_row + jnp.sum(p_n, axis=0, keepdims=True)
            p_sc[p_slot, n * bs:(n + 1) * bs, :] = p_n.astype(BF16)
        o_t = jnp.dot(vt_ref[0, hh * head_dim:(hh + 1) * head_dim, 0:(i + 1) * bs],
                      p_sc[p_slot, 0:(i + 1) * bs, :], preferred_element_type=F32)
        ot_sc[hh * head_dim:(hh + 1) * head_dim, i * bs:(i + 1) * bs] = o_t / l_row
        if hh == heads - 1:
            o_ref[0, i * bs:(i + 1) * bs, :] = ot_sc[:, i * bs:(i + 1) * bs].T.astype(BF16)

    items = [(i, hh) for i in reversed(range(nb)) for hh in range(heads)]
    ahead = t_sc.shape[0] - 1
    pending = [scores(*item, slot) for slot, item in enumerate(items[:ahead])]
    for idx, item in enumerate(items):
        if idx + ahead < len(items):
            pending.append(scores(*items[idx + ahead], (idx + ahead) % (ahead + 1)))
        finish(*item, idx % (ahead + 1), idx % p_sc.shape[0], pending.pop(0))


def _moba(q, k, vt, kmean, *, batch, seq, d):
    nb = seq // MOBA_BLOCK
    pairs = d // LANES
    head_dim = d // N_HEADS
    slopes = LOG2E * jnp.exp2(-8.0 * jnp.arange(1, N_HEADS + 1, dtype=F32) / N_HEADS)
    grid_spec = pltpu.PrefetchScalarGridSpec(
        num_scalar_prefetch=1,
        grid=(batch * pairs,),
        in_specs=[
            pl.BlockSpec((1, seq, LANES), lambda bp, s: (bp // pairs, 0, bp % pairs)),
            pl.BlockSpec((1, seq, LANES), lambda bp, s: (bp // pairs, 0, bp % pairs)),
            pl.BlockSpec((1, LANES, seq), lambda bp, s: (bp // pairs, bp % pairs, 0)),
            pl.BlockSpec((1, nb, LANES), lambda bp, s: (bp // pairs, 0, bp % pairs)),
        ],
        out_specs=pl.BlockSpec((1, seq, LANES), lambda bp, s: (bp // pairs, 0, bp % pairs)),
        scratch_shapes=[
            pltpu.VMEM((LANES, seq), F32),
            pltpu.VMEM((MOBA_AHEAD + 1, seq, MOBA_BLOCK), F32),
            pltpu.VMEM((2, seq, MOBA_BLOCK), BF16),
        ],
    )
    o = pl.pallas_call(
        functools.partial(_moba_kernel, nb=nb, head_dim=head_dim, pairs=pairs),
        out_shape=jax.ShapeDtypeStruct((batch, seq, d), BF16),
        grid_spec=grid_spec,
        compiler_params=_params(("arbitrary",)),
        name="moba",
    )(slopes, q.reshape(batch, seq, d), k, vt, kmean)
    return o.reshape(batch * seq, d)


def kernel(x, ffn1_norm, ffn1_w_in, ffn1_w_out, mix_norm, ffn2_norm, ffn2_w_in, ffn2_w_out,
           s5_a_re, s5_a_im, s5_b_re, s5_b_im, s5_c_re, s5_c_im, s5_d, s5_log_step, s5_w_glu,
           kv_norm, w_kv, w_q, w_o, final_norm):
    batch, seq, d = x.shape
    depth = ffn1_norm.shape[0]
    n_a = s5_a_re.shape[0]
    h = x.reshape(batch * seq, d)
    s5_params = _s5_discretize(s5_a_re, s5_a_im, s5_b_re, s5_b_im, s5_c_re, s5_c_im, s5_log_step, batch)
    w_glu = s5_w_glu.astype(BF16)
    k = vt = kmean = None
    for layer in range(depth):
        if layer == n_a:
            k, vt, kmean = _kv_proj(h, kv_norm, w_kv, batch=batch)
        if layer < n_a:
            h = _ffn(h, ffn1_norm[layer], ffn1_w_in, ffn1_w_out, layer)
            h = _s5_mixer(h, mix_norm[layer], s5_params, s5_d[layer], w_glu, layer, batch=batch)
            attn = None
        else:
            j = layer - n_a
            h, q = _ffn(h, ffn1_norm[layer], ffn1_w_in, ffn1_w_out, layer,
                        q_proj=(mix_norm[layer], w_q[j], LOG2E * (d // N_HEADS) ** -0.5))
            attn = (_moba(q, k, vt, kmean, batch=batch, seq=seq, d=d), w_o[j])
        last = layer == depth - 1
        h = _ffn(h, ffn2_norm[layer], ffn2_w_in, ffn2_w_out, layer, attn=attn, final_g=final_norm if last else None)
    return h.reshape(batch, seq, d)
```

```python
import functools

import jax
import jax.numpy as jnp
from jax import lax
from jax.experimental import pallas as pl
from jax.experimental.pallas import tpu as pltpu

S5_GROUP = 16
S5_STATE = 64
N_HEADS = 16
MOBA_BLOCK = 256
MOBA_TOPK = 3
EPS = 1e-6
NEG = -1e30
LOG2E = 1.4426950408889634
MOBA_AHEAD = 3

LANES = 128
S5_CHUNK = LANES
S5_CHUNK_STATES = S5_CHUNK // S5_GROUP * S5_STATE
VMEM_LIMIT = 56 * 1024 * 1024

BF16 = jnp.bfloat16
F32 = jnp.float32
NT = (((1,), (1,)), ((), ()))


def _rms(x, g):
    return x * lax.rsqrt(jnp.mean(x * x, axis=-1, keepdims=True) + EPS) * g


def _const_spec(shape, index=None):
    index = (0,) * len(shape) if index is None else index
    return pl.BlockSpec(shape, lambda *_: index, pipeline_mode=pl.Buffered(1))


def _params(semantics):
    return pltpu.CompilerParams(dimension_semantics=semantics, vmem_limit_bytes=VMEM_LIMIT)


def _ffn_kernel(*refs, d_ff, ck, n_w, attn_in, q_out, final_norm, q_scale):
    refs = list(refs)
    x_ref = refs.pop(0)
    attn_ref, wo_ref = (refs.pop(0), refs.pop(0)) if attn_in else (None, None)
    g_ref, win_ref, wout_ref = refs.pop(0), refs.pop(0), refs.pop(0)
    gq_ref, wq_ref = (refs.pop(0), refs.pop(0)) if q_out else (None, None)
    fg_ref = refs.pop(0) if final_norm else None
    o_ref = refs.pop(0)
    q_ref = refs.pop(0) if q_out else None
    win_sc, wout_sc = refs
    step = pl.program_id(0)

    @pl.when(step < n_w)
    def _():
        r_in, r_out = win_ref.shape[0], wout_ref.shape[0]
        win_sc[pl.ds(pl.multiple_of(step * r_in, r_in), r_in), :] = win_ref[...].astype(BF16)
        wout_sc[pl.ds(pl.multiple_of(step * r_out, r_out), r_out), :] = wout_ref[...].astype(BF16)

    @pl.when(step >= n_w)
    def _():
        x = x_ref[...]
        if attn_in:
            x = x + jnp.dot(attn_ref[...], wo_ref[...], preferred_element_type=F32)
        xn = _rms(x, g_ref[...]).astype(BF16)
        acc = jnp.zeros(x.shape, F32)
        for c in range(d_ff // ck):
            gate = jnp.dot(xn, win_sc[:, c * ck:(c + 1) * ck], preferred_element_type=F32)
            up = jnp.dot(xn, win_sc[:, d_ff + c * ck:d_ff + (c + 1) * ck], preferred_element_type=F32)
            a = (gate * jax.nn.sigmoid(gate) * up).astype(BF16)
            acc = acc + jnp.dot(a, wout_sc[c * ck:(c + 1) * ck, :], preferred_element_type=F32)
        y = x + 0.5 * acc
        if q_out:
            yn = _rms(y, gq_ref[...]).astype(BF16)
            q_ref[...] = (jnp.dot(yn, wq_ref[...], preferred_element_type=F32) * q_scale).astype(BF16)
        if final_norm:
            y = _rms(y, fg_ref[...])
        o_ref[...] = y


def _ffn(h, g, w_in, w_out, layer, *, attn=None, q_proj=None, final_g=None, tm=1024, ck=256, n_w=16):
    t, d = h.shape
    d_ff = w_out.shape[1]
    row_spec = pl.BlockSpec((tm, d), lambda i: (jnp.maximum(i - n_w, 0), 0))
    slab = lambda i: (layer, jnp.minimum(i, n_w - 1), 0)
    args, in_specs = [h], [row_spec]
    if attn is not None:
        args += [attn[0], attn[1].astype(BF16)]
        in_specs += [row_spec, _const_spec((d, d))]
    args += [g.reshape(1, d), w_in, w_out]
    in_specs += [_const_spec((1, d)), pl.BlockSpec((None, d // n_w, 2 * d_ff), slab),
                 pl.BlockSpec((None, d_ff // n_w, d), slab)]
    out_shape, out_specs = [jax.ShapeDtypeStruct((t, d), F32)], [row_spec]
    if q_proj is not None:
        args += [q_proj[0].reshape(1, d), q_proj[1].astype(BF16)]
        in_specs += [_const_spec((1, d)), _const_spec((d, d))]
        out_shape.append(jax.ShapeDtypeStruct((t, d), BF16))
        out_specs.append(row_spec)
    if final_g is not None:
        args.append(final_g.reshape(1, d))
        in_specs.append(_const_spec((1, d)))
    out = pl.pallas_call(
        functools.partial(_ffn_kernel, d_ff=d_ff, ck=ck, n_w=n_w, attn_in=attn is not None,
                          q_out=q_proj is not None, final_norm=final_g is not None,
                          q_scale=q_proj[2] if q_proj else None),
        out_shape=out_shape,
        grid=(n_w + t // tm,),
        in_specs=in_specs,
        out_specs=out_specs,
        scratch_shapes=[pltpu.VMEM((d, 2 * d_ff), BF16), pltpu.VMEM((d_ff, d), BF16)],
        compiler_params=_params(("arbitrary",)),
        name="ffn",
    )(*args)
    return out if q_proj is not None else out[0]


def _to_time_major(x, slab_sc, tl, batch):
    slabs = x.shape[1] // LANES
    for s in range(slabs):
        for b in range(batch):
            slab_sc[s, pl.ds(b, tl, stride=batch), :] = x[b * tl:(b + 1) * tl, s * LANES:(s + 1) * LANES]
    return jnp.concatenate([slab_sc[s] for s in range(slabs)], axis=-1)


def _to_batch_major(y, slab_sc, tl, batch):
    slabs = y.shape[1] // LANES
    for s in range(slabs):
        slab_sc[s] = y[:, s * LANES:(s + 1) * LANES]
    return jnp.concatenate(
        [jnp.concatenate([slab_sc[s, pl.ds(b, tl, stride=batch), :] for s in range(slabs)], axis=-1)
         for b in range(batch)], axis=0)


def _s5_pack_operators(bq_ref, cq_ref, cbq_ref, b2_ref, c2_ref, cb_ref, n_chunks):
    def replicate(x, reps):
        w = x.shape[1]
        sel = (lax.broadcasted_iota(jnp.int32, (w, w * reps), 1) % w
               == lax.broadcasted_iota(jnp.int32, (w, w * reps), 0)).astype(BF16)
        return jnp.dot(x.astype(BF16), sel, preferred_element_type=F32)

    def own_group(shape, rows_per_group, lanes_per_group):
        return (lax.broadcasted_iota(jnp.int32, shape, 0) // rows_per_group
                == lax.broadcasted_iota(jnp.int32, shape, 1) // lanes_per_group)

    gc = S5_CHUNK // S5_GROUP
    for j in range(n_chunks):
        keep = own_group((gc * S5_STATE, S5_CHUNK), S5_STATE, S5_GROUP)
        b2t = jnp.concatenate(
            [jnp.concatenate([jnp.where(keep, replicate(bq_ref[2 * r + q, j], gc), 0.0) for r in range(2)], axis=1)
             for q in range(2)], axis=0)
        b2_ref[j] = b2t.T.astype(BF16)
        keep = own_group((S5_CHUNK, gc * S5_STATE), S5_GROUP, S5_STATE)
        c2t = jnp.concatenate(
            [jnp.concatenate([jnp.where(keep, replicate(cq_ref[2 * s + q, j], gc), 0.0) for q in range(2)], axis=1)
             for s in range(2)], axis=0)
        c2_ref[j] = c2t.T.astype(BF16)
        keep = own_group((S5_CHUNK, S5_CHUNK), S5_GROUP, S5_GROUP)
        cb_ref[j] = jnp.where(keep, replicate(cbq_ref[j], gc), 0.0).astype(BF16)


def _s5_kernel(h_ref, hprev_ref, g_ref, lam_ref, bq_ref, cq_ref, cbq_ref, dskip_ref, wglu_ref, o_ref,
               state_sc, ynext_sc, z_sc, slab_in_sc, slab_out_sc, b2_ref, c2_ref, cb_ref, *,
               tl, n_chunks, batch):
    @pl.when(pl.program_id(0) == 0)
    def _():
        state_sc[...] = jnp.zeros(state_sc.shape, F32)
        ynext_sc[...] = jnp.zeros(ynext_sc.shape, F32)
        z_sc[...] = jnp.zeros(z_sc.shape, BF16)
        _s5_pack_operators(bq_ref, cq_ref, cbq_ref, b2_ref, c2_ref, cb_ref, n_chunks)

    ns = S5_CHUNK_STATES
    d = h_ref.shape[-1]
    half = tl // 2
    glu_cols = 2 * d // n_chunks
    z_prev = z_sc[...]
    hn = _to_time_major(_rms(h_ref[...].reshape(batch * tl, d), g_ref[...]), slab_in_sc, tl, batch)
    hn = hn.reshape(half, 2 * batch, d)
    u_e = hn[:, :batch].reshape(half * batch, d)
    u_o = hn[:, batch:].reshape(half * batch, d)
    ue_b, uo_b = u_e.astype(BF16), u_o.astype(BF16)
    def input_proj(j):
        cols = slice(j * S5_CHUNK, (j + 1) * S5_CHUNK)
        return jnp.dot(jnp.concatenate([uo_b[:, cols], ue_b[:, cols]], axis=-1), b2_ref[j],
                       preferred_element_type=F32)

    z_e, z_o, zz = [], [], []
    bu_next = input_proj(0)
    for j in range(n_chunks):
        cols = slice(j * S5_CHUNK, (j + 1) * S5_CHUNK)
        bu = bu_next
        if j + 1 < n_chunks:
            bu_next = input_proj(j + 1)
        zz.append(jnp.dot(z_prev, wglu_ref[:, j * glu_cols:(j + 1) * glu_cols], preferred_element_type=F32))
        lam = lam_ref[j]
        lr, li = lam[:, :ns], lam[:, ns:]
        x = state_sc[j]
        xs = []
        for m in range(half):
            bu_m = bu[m * batch:(m + 1) * batch]
            xr, xi = x[:, :ns], x[:, ns:]
            nr = lr * xr - li * xi + bu_m[:, :ns]
            ni = lr * xi + li * xr + bu_m[:, ns:]
            x = jnp.concatenate([nr, ni], axis=-1)
            xs.append(x)
        state_sc[j] = x
        y2 = jnp.dot(jnp.concatenate(xs, axis=0).astype(BF16), c2_ref[j], preferred_element_type=F32)
        y_o, y_next = y2[:, :S5_CHUNK], y2[:, S5_CHUNK:]
        y_e = jnp.concatenate([ynext_sc[j], y_next[:-batch]], axis=0)
        ynext_sc[j] = y_next[-batch:]
        y_e = y_e + jnp.dot(ue_b[:, cols], cb_ref[j], preferred_element_type=F32)
        z_e.append(jax.nn.gelu(y_e + dskip_ref[:, cols] * u_e[:, cols]).astype(BF16))
        z_o.append(jax.nn.gelu(y_o + dskip_ref[:, cols] * u_o[:, cols]).astype(BF16))
    z_sc[...] = jnp.concatenate([jnp.concatenate(z_e, axis=-1), jnp.concatenate(z_o, axis=-1)], axis=0)
    zz = jnp.concatenate(zz, axis=-1)
    mixed = zz[:, :d] * jax.nn.sigmoid(zz[:, d:])
    mixed = jnp.concatenate([mixed[:half * batch].reshape(half, batch, d),
                             mixed[half * batch:].reshape(half, batch, d)], axis=1).reshape(tl * batch, d)
    mixed = _to_batch_major(mixed, slab_out_sc, tl, batch)
    o_ref[...] = hprev_ref[...] + mixed.reshape(batch, tl, d)


def _s5_discretize(a_re, a_im, b_re, b_im, c_re, c_im, log_step, batch):
    n, g, p = a_re.shape
    gc = S5_CHUNK // S5_GROUP
    n_chunks = g // gc
    dt = jnp.exp(log_step)[..., None]
    ar = jnp.minimum(a_re, -1e-4)
    ai = a_im
    mag = jnp.exp(ar * dt)
    lr = mag * jnp.cos(ai * dt)
    li = mag * jnp.sin(ai * dt)
    nr = lr - 1.0
    den = ar * ar + ai * ai
    fr = (nr * ar + li * ai) / den
    fi = (li * ar - nr * ai) / den
    bbr = fr[..., None] * b_re - fi[..., None] * b_im
    bbi = fr[..., None] * b_im + fi[..., None] * b_re
    lbr = lr[..., None] * bbr - li[..., None] * bbi
    lbi = lr[..., None] * bbi + li[..., None] * bbr
    clr = c_re * lr[:, :, None, :] - c_im * li[:, :, None, :]
    cli = c_re * li[:, :, None, :] + c_im * lr[:, :, None, :]
    cb = jnp.einsum('ngjp,ngpk->ngkj', c_re, bbr) - jnp.einsum('ngjp,ngpk->ngkj', c_im, bbi)
    bq = jnp.stack([bbr, bbi, lbr, lbi]).reshape(4, n, n_chunks, gc * p, S5_GROUP)
    cq = jnp.stack([c_re, -c_im, clr, -cli]).reshape(4, n, n_chunks, S5_CHUNK, p)
    cbq = cb.reshape(n, n_chunks, S5_CHUNK, S5_GROUP)
    lam2 = jnp.concatenate([(lr * lr - li * li).reshape(n, n_chunks, 1, gc * p),
                            (2.0 * lr * li).reshape(n, n_chunks, 1, gc * p)], axis=-1)
    lam2 = jnp.broadcast_to(lam2, (n, n_chunks, batch, 2 * gc * p))
    return lam2, bq, cq, cbq


def _s5_mixer(h, g, s5_params, d_skip, w_glu, layer, *, batch, tl=64):
    t, d = h.shape
    seq = t // batch
    n_chunks = d // S5_CHUNK
    ns2 = 2 * S5_CHUNK_STATES
    rows = tl * batch
    nt = seq // tl
    h3 = h.reshape(batch, seq, d)
    out = pl.pallas_call(
        functools.partial(_s5_kernel, tl=tl, n_chunks=n_chunks, batch=batch),
        out_shape=jax.ShapeDtypeStruct((batch, seq, d), F32),
        grid=(nt + 1,),
        in_specs=[
            pl.BlockSpec((batch, tl, d), lambda i: (0, jnp.minimum(i, nt - 1), 0)),
            pl.BlockSpec((batch, tl, d), lambda i: (0, jnp.maximum(i - 1, 0), 0)),
            _const_spec((1, d)),
            _const_spec((None, n_chunks, batch, ns2), (layer, 0, 0, 0)),
            _const_spec((4, None, n_chunks, S5_CHUNK_STATES, S5_GROUP), (0, layer, 0, 0, 0)),
            _const_spec((4, None, n_chunks, S5_CHUNK, S5_STATE), (0, layer, 0, 0, 0)),
            _const_spec((None, n_chunks, S5_CHUNK, S5_GROUP), (layer, 0, 0, 0)),
            _const_spec((1, d)),
            _const_spec((None, d, 2 * d), (layer, 0, 0)),
        ],
        out_specs=pl.BlockSpec((batch, tl, d), lambda i: (0, jnp.maximum(i - 1, 0), 0)),
        scratch_shapes=[
            pltpu.VMEM((n_chunks, batch, ns2), F32),
            pltpu.VMEM((n_chunks, batch, S5_CHUNK), F32),
            pltpu.VMEM((rows, d), BF16),
            pltpu.VMEM((d // LANES, rows, LANES), F32),
            pltpu.VMEM((d // LANES, rows, LANES), F32),
            pltpu.VMEM((n_chunks, 2 * S5_CHUNK, ns2), BF16),
            pltpu.VMEM((n_chunks, ns2, 2 * S5_CHUNK), BF16),
            pltpu.VMEM((n_chunks, S5_CHUNK, S5_CHUNK), BF16),
        ],
        compiler_params=_params(("arbitrary",)),
        name="s5_mixer",
    )(h3, h3, g.reshape(1, d), *s5_params, d_skip.reshape(1, d), w_glu)
    return out.reshape(t, d)


def _kv_kernel(x_ref, g_ref, wk_ref, wvt_ref, k_ref, vt_ref, km_ref, *, ck):
    xn = _rms(x_ref[0], g_ref[...]).astype(BF16)
    seq, d = xn.shape
    nb = seq // MOBA_BLOCK
    for c in range(d // ck):
        cols = slice(c * ck, (c + 1) * ck)
        kc = jnp.dot(xn, wk_ref[:, cols], preferred_element_type=F32)
        k_ref[0, :, cols] = kc.astype(BF16)
        km_ref[0, :, cols] = jnp.mean(kc.reshape(nb, MOBA_BLOCK, ck), axis=1)
        vt_ref[0, cols, :] = lax.dot_general(wvt_ref[cols, :], xn, NT,
                                             preferred_element_type=F32).astype(BF16)


def _kv_proj(h, g, w_kv, *, batch, ck=256):
    t, d = h.shape
    seq = t // batch
    nb = seq // MOBA_BLOCK
    w_k = w_kv[:, :d].astype(BF16)
    w_vt = w_kv[:, d:].T.astype(BF16)
    return pl.pallas_call(
        functools.partial(_kv_kernel, ck=ck),
        out_shape=(jax.ShapeDtypeStruct((batch, seq, d), BF16),
                   jax.ShapeDtypeStruct((batch, d, seq), BF16),
                   jax.ShapeDtypeStruct((batch, nb, d), F32)),
        grid=(batch,),
        in_specs=[
            pl.BlockSpec((1, seq, d), lambda b: (b, 0, 0)),
            _const_spec((1, d)),
            _const_spec((d, d)),
            _const_spec((d, d)),
        ],
        out_specs=(pl.BlockSpec((1, seq, d), lambda b: (b, 0, 0)),
                   pl.BlockSpec((1, d, seq), lambda b: (b, 0, 0)),
                   pl.BlockSpec((1, nb, d), lambda b: (b, 0, 0))),
        compiler_params=_params(("arbitrary",)),
        name="kv_proj",
    )(h.reshape(batch, seq, d), g.reshape(1, d), w_k, w_vt)


def _moba_kernel(slopes_ref, q_ref, k_ref, vt_ref, km_ref, o_ref, ot_sc, t_sc, p_sc, *,
                 nb, head_dim, pairs):
    bs = MOBA_BLOCK
    pair = pl.program_id(0) % pairs
    heads = LANES // head_dim
    seq = q_ref.shape[1]
    q = q_ref[0].reshape(nb, bs, LANES)
    k = k_ref[0].reshape(nb, bs, LANES)
    km = km_ref[0].astype(BF16)
    lane = lax.broadcasted_iota(jnp.int32, (1, LANES), 1)
    pos = lax.broadcasted_iota(jnp.int32, (bs, LANES), 0).astype(F32)
    causal = (lax.broadcasted_iota(jnp.int32, (bs, bs), 1)
              >= lax.broadcasted_iota(jnp.int32, (bs, bs), 0))
    blk = lax.broadcasted_iota(jnp.int32, (nb, bs), 0)

    def split3(x):
        hi = x.astype(BF16).astype(F32)
        mid = (x - hi).astype(BF16).astype(F32)
        lo = (x - hi - mid).astype(BF16).astype(F32)
        return hi, mid, lo

    lane_b = lax.broadcasted_iota(jnp.int32, (bs, LANES), 1)

    def lanes_of(terms, base):
        out = jnp.zeros((bs, LANES), F32)
        for j, term in enumerate(terms):
            out = jnp.where(lane_b == base + j, term, out)
        return out.astype(BF16)

    slope, gate_t, q_aug, k_aug = [], [], [], []
    for hh in range(heads):
        slope.append(slopes_ref[pair * heads + hh])
        in_head = lane // head_dim == hh
        fb = ((hh + 1) % heads) * head_dim
        slope_v = jnp.full((bs, LANES), slope[hh], F32)
        q_feat = lanes_of(split3(slope_v) + split3(-slope_v * pos), fb)
        k_feat = lanes_of([pos] * 3 + [jnp.ones((bs, LANES), F32)] * 3, fb)
        q_aug.append(jnp.where(in_head, q, q_feat[None]).reshape(seq, LANES))
        k_aug.append(jnp.where(in_head, k, k_feat[None]).reshape(seq, LANES))
        gate_t.append(lax.dot_general(jnp.where(in_head, km, jnp.zeros_like(km)), q_aug[hh], NT,
                                      preferred_element_type=F32))

    def scores(i, hh, slot):
        s = lax.dot_general(k_aug[hh][0:(i + 1) * bs], q_aug[hh][i * bs:(i + 1) * bs], NT,
                            preferred_element_type=F32)
        col_max = []
        for n in range(i + 1):
            t_n = s[n * bs:(n + 1) * bs]
            if n == i:
                t_n = jnp.where(causal, t_n, NEG)
            t_sc[slot, n * bs:(n + 1) * bs, :] = t_n
            col_max.append(jnp.max(t_n, axis=0, keepdims=True))
        return col_max

    def finish(i, hh, slot, p_slot, col_max):
        if i > MOBA_TOPK:
            g = gate_t[hh][:, i * bs:(i + 1) * bs]
            rank = jnp.zeros((nb, bs), jnp.int32)
            for m in range(i):
                gm = g[m:m + 1, :]
                rank = rank + ((gm > g) | ((gm == g) & (m < blk))).astype(jnp.int32)
            sel = rank < MOBA_TOPK
        bias = []
        for n in range(i):
            b_n = -slope[hh] * float((i - n) * bs)
            if i > MOBA_TOPK:
                b_n = jnp.where(sel[n:n + 1, :], b_n, NEG)
            bias.append(b_n)
        bias.append(0.0)
        m_row = col_max[i]
        for n in range(i):
            m_row = jnp.maximum(m_row, col_max[n] + bias[n])
        l_row = jnp.zeros((1, bs), F32)
        for n in range(i + 1):
            p_n = jnp.exp2(t_sc[slot, n * bs:(n + 1) * bs, :] - (m_row - bias[n]))
            l_row = l_row + jnp.sum(p_n, axis=0, keepdims=True)
            p_sc[p_slot, n * bs:(n + 1) * bs, :] = p_n.astype(BF16)
        o_t = jnp.dot(vt_ref[0, hh * head_dim:(hh + 1) * head_dim, 0:(i + 1) * bs],
                      p_sc[p_slot, 0:(i + 1) * bs, :], preferred_element_type=F32)
        ot_sc[hh * head_dim:(hh + 1) * head_dim, i * bs:(i + 1) * bs] = o_t / l_row
        if hh == heads - 1:
            o_ref[0, i * bs:(i + 1) * bs, :] = ot_sc[:, i * bs:(i + 1) * bs].T.astype(BF16)

    items = [(i, hh) for i in reversed(range(nb)) for hh in range(heads)]
    ahead = t_sc.shape[0] - 1
    pending = [scores(*item, slot) for slot, item in enumerate(items[:ahead])]
    for idx, item in enumerate(items):
        if idx + ahead < len(items):
            pending.append(scores(*items[idx + ahead], (idx + ahead) % (ahead + 1)))
        finish(*item, idx % (ahead + 1), idx % p_sc.shape[0], pending.pop(0))


def _moba(q, k, vt, kmean, *, batch, seq, d):
    nb = seq // MOBA_BLOCK
    pairs = d // LANES
    head_dim = d // N_HEADS
    slopes = LOG2E * jnp.exp2(-8.0 * jnp.arange(1, N_HEADS + 1, dtype=F32) / N_HEADS)
    grid_spec = pltpu.PrefetchScalarGridSpec(
        num_scalar_prefetch=1,
        grid=(batch * pairs,),
        in_specs=[
            pl.BlockSpec((1, seq, LANES), lambda bp, s: (bp // pairs, 0, bp % pairs)),
            pl.BlockSpec((1, seq, LANES), lambda bp, s: (bp // pairs, 0, bp % pairs)),
            pl.BlockSpec((1, LANES, seq), lambda bp, s: (bp // pairs, bp % pairs, 0)),
            pl.BlockSpec((1, nb, LANES), lambda bp, s: (bp // pairs, 0, bp % pairs)),
        ],
        out_specs=pl.BlockSpec((1, seq, LANES), lambda bp, s: (bp // pairs, 0, bp % pairs)),
        scratch_shapes=[
            pltpu.VMEM((LANES, seq), F32),
            pltpu.VMEM((MOBA_AHEAD + 1, seq, MOBA_BLOCK), F32),
            pltpu.VMEM((2, seq, MOBA_BLOCK), BF16),
        ],
    )
    o = pl.pallas_call(
        functools.partial(_moba_kernel, nb=nb, head_dim=head_dim, pairs=pairs),
        out_shape=jax.ShapeDtypeStruct((batch, seq, d), BF16),
        grid_spec=grid_spec,
        compiler_params=_params(("arbitrary",)),
        name="moba",
    )(slopes, q.reshape(batch, seq, d), k, vt, kmean)
    return o.reshape(batch * seq, d)


def kernel(x, ffn1_norm, ffn1_w_in, ffn1_w_out, mix_norm, ffn2_norm, ffn2_w_in, ffn2_w_out,
           s5_a_re, s5_a_im, s5_b_re, s5_b_im, s5_c_re, s5_c_im, s5_d, s5_log_step, s5_w_glu,
           kv_norm, w_kv, w_q, w_o, final_norm):
    batch, seq, d = x.shape
    depth = ffn1_norm.shape[0]
    n_a = s5_a_re.shape[0]
    h = x.reshape(batch * seq, d)
    s5_params = _s5_discretize(s5_a_re, s5_a_im, s5_b_re, s5_b_im, s5_c_re, s5_c_im, s5_log_step, batch)
    w_glu = s5_w_glu.astype(BF16)
    k = vt = kmean = None
    for layer in range(depth):
        if layer == n_a:
            k, vt, kmean = _kv_proj(h, kv_norm, w_kv, batch=batch)
        if layer < n_a:
            h = _ffn(h, ffn1_norm[layer], ffn1_w_in, ffn1_w_out, layer)
            h = _s5_mixer(h, mix_norm[layer], s5_params, s5_d[layer], w_glu, layer, batch=batch)
            attn = None
        else:
            j = layer - n_a
            h, q = _ffn(h, ffn1_norm[layer], ffn1_w_in, ffn1_w_out, layer,
                        q_proj=(mix_norm[layer], w_q[j], LOG2E * (d // N_HEADS) ** -0.5))
            attn = (_moba(q, k, vt, kmean, batch=batch, seq=seq, d=d), w_o[j])
        last = layer == depth - 1
        h = _ffn(h, ffn2_norm[layer], ffn2_w_in, ffn2_w_out, layer, attn=attn, final_g=final_norm if last else None)
    return h.reshape(batch, seq, d)
```

```python
import functools

import jax
import jax.numpy as jnp
from jax import lax
from jax.experimental import pallas as pl
from jax.experimental.pallas import tpu as pltpu

S5_GROUP = 16
S5_STATE = 64
N_HEADS = 16
MOBA_BLOCK = 256
MOBA_TOPK = 3
EPS = 1e-6
NEG = -1e30
LOG2E = 1.4426950408889634
MOBA_AHEAD = 5

LANES = 128
S5_CHUNK = LANES
S5_CHUNK_STATES = S5_CHUNK // S5_GROUP * S5_STATE
VMEM_LIMIT = 56 * 1024 * 1024

BF16 = jnp.bfloat16
F32 = jnp.float32
NT = (((1,), (1,)), ((), ()))


def _rms(x, g):
    return x * lax.rsqrt(jnp.mean(x * x, axis=-1, keepdims=True) + EPS) * g


def _const_spec(shape, index=None):
    index = (0,) * len(shape) if index is None else index
    return pl.BlockSpec(shape, lambda *_: index, pipeline_mode=pl.Buffered(1))


def _params(semantics):
    return pltpu.CompilerParams(dimension_semantics=semantics, vmem_limit_bytes=VMEM_LIMIT)


def _ffn_kernel(*refs, d_ff, ck, n_w, attn_in, q_out, final_norm, q_scale):
    refs = list(refs)
    x_ref = refs.pop(0)
    attn_ref, wo_ref = (refs.pop(0), refs.pop(0)) if attn_in else (None, None)
    g_ref, win_ref, wout_ref = refs.pop(0), refs.pop(0), refs.pop(0)
    gq_ref, wq_ref = (refs.pop(0), refs.pop(0)) if q_out else (None, None)
    fg_ref = refs.pop(0) if final_norm else None
    o_ref = refs.pop(0)
    q_ref = refs.pop(0) if q_out else None
    win_sc, wout_sc = refs
    step = pl.program_id(0)

    @pl.when(step < n_w)
    def _():
        r_in, r_out = win_ref.shape[0], wout_ref.shape[0]
        win_sc[pl.ds(pl.multiple_of(step * r_in, r_in), r_in), :] = win_ref[...].astype(BF16)
        wout_sc[pl.ds(pl.multiple_of(step * r_out, r_out), r_out), :] = wout_ref[...].astype(BF16)

    @pl.when(step >= n_w)
    def _():
        x = x_ref[...]
        if attn_in:
            x = x + jnp.dot(attn_ref[...], wo_ref[...], preferred_element_type=F32)
        xn = _rms(x, g_ref[...]).astype(BF16)
        acc = jnp.zeros(x.shape, F32)
        for c in range(d_ff // ck):
            gate = jnp.dot(xn, win_sc[:, c * ck:(c + 1) * ck], preferred_element_type=F32)
            up = jnp.dot(xn, win_sc[:, d_ff + c * ck:d_ff + (c + 1) * ck], preferred_element_type=F32)
            a = (gate * jax.nn.sigmoid(gate) * up).astype(BF16)
            acc = acc + jnp.dot(a, wout_sc[c * ck:(c + 1) * ck, :], preferred_element_type=F32)
        y = x + 0.5 * acc
        if q_out:
            yn = _rms(y, gq_ref[...]).astype(BF16)
            q_ref[...] = (jnp.dot(yn, wq_ref[...], preferred_element_type=F32) * q_scale).astype(BF16)
        if final_norm:
            y = _rms(y, fg_ref[...])
        o_ref[...] = y


def _ffn(h, g, w_in, w_out, layer, *, attn=None, q_proj=None, final_g=None, tm=1024, ck=256, n_w=16):
    t, d = h.shape
    d_ff = w_out.shape[1]
    bf16_rows = 16
    assert t % tm == 0 and d_ff % ck == 0 and d % (n_w * bf16_rows) == 0 and d_ff % (n_w * bf16_rows) == 0
    row_spec = pl.BlockSpec((tm, d), lambda i: (jnp.maximum(i - n_w, 0), 0))
    slab = lambda i: (layer, jnp.minimum(i, n_w - 1), 0)
    args, in_specs = [h], [row_spec]
    if attn is not None:
        args += [attn[0], attn[1].astype(BF16)]
        in_specs += [row_spec, _const_spec((d, d))]
    args += [g.reshape(1, d), w_in, w_out]
    in_specs += [_const_spec((1, d)), pl.BlockSpec((None, d // n_w, 2 * d_ff), slab),
                 pl.BlockSpec((None, d_ff // n_w, d), slab)]
    out_shape, out_specs = [jax.ShapeDtypeStruct((t, d), F32)], [row_spec]
    if q_proj is not None:
        args += [q_proj[0].reshape(1, d), q_proj[1].astype(BF16)]
        in_specs += [_const_spec((1, d)), _const_spec((d, d))]
        out_shape.append(jax.ShapeDtypeStruct((t, d), BF16))
        out_specs.append(row_spec)
    if final_g is not None:
        args.append(final_g.reshape(1, d))
        in_specs.append(_const_spec((1, d)))
    out = pl.pallas_call(
        functools.partial(_ffn_kernel, d_ff=d_ff, ck=ck, n_w=n_w, attn_in=attn is not None,
                          q_out=q_proj is not None, final_norm=final_g is not None,
                          q_scale=q_proj[2] if q_proj else None),
        out_shape=out_shape,
        grid=(n_w + t // tm,),
        in_specs=in_specs,
        out_specs=out_specs,
        scratch_shapes=[pltpu.VMEM((d, 2 * d_ff), BF16), pltpu.VMEM((d_ff, d), BF16)],
        compiler_params=_params(("arbitrary",)),
        name="ffn",
    )(*args)
    return out if q_proj is not None else out[0]


def _to_time_major(x, slab_sc, tl, batch):
    slabs = x.shape[1] // LANES
    for s in range(slabs):
        for b in range(batch):
            slab_sc[s, pl.ds(b, tl, stride=batch), :] = x[b * tl:(b + 1) * tl, s * LANES:(s + 1) * LANES]
    return jnp.concatenate([slab_sc[s] for s in range(slabs)], axis=-1)


def _to_batch_major(y, slab_sc, tl, batch):
    slabs = y.shape[1] // LANES
    for s in range(slabs):
        slab_sc[s] = y[:, s * LANES:(s + 1) * LANES]
    return jnp.concatenate(
        [jnp.concatenate([slab_sc[s, pl.ds(b, tl, stride=batch), :] for s in range(slabs)], axis=-1)
         for b in range(batch)], axis=0)


def _s5_pack_operators(bq_ref, cq_ref, cbq_ref, b2_ref, c2_ref, cb_ref, n_chunks):
    def replicate(x, reps):
        w = x.shape[1]
        sel = (lax.broadcasted_iota(jnp.int32, (w, w * reps), 1) % w
               == lax.broadcasted_iota(jnp.int32, (w, w * reps), 0)).astype(BF16)
        return jnp.dot(x.astype(BF16), sel, preferred_element_type=F32)

    def own_group(shape, rows_per_group, lanes_per_group):
        return (lax.broadcasted_iota(jnp.int32, shape, 0) // rows_per_group
                == lax.broadcasted_iota(jnp.int32, shape, 1) // lanes_per_group)

    gc = S5_CHUNK // S5_GROUP
    for j in range(n_chunks):
        keep = own_group((gc * S5_STATE, S5_CHUNK), S5_STATE, S5_GROUP)
        b2t = jnp.concatenate(
            [jnp.concatenate([jnp.where(keep, replicate(bq_ref[2 * r + q, j], gc), 0.0) for r in range(2)], axis=1)
             for q in range(2)], axis=0)
        b2_ref[j] = b2t.T.astype(BF16)
        keep = own_group((S5_CHUNK, gc * S5_STATE), S5_GROUP, S5_STATE)
        c2t = jnp.concatenate(
            [jnp.concatenate([jnp.where(keep, replicate(cq_ref[2 * s + q, j], gc), 0.0) for q in range(2)], axis=1)
             for s in range(2)], axis=0)
        c2_ref[j] = c2t.T.astype(BF16)
        keep = own_group((S5_CHUNK, S5_CHUNK), S5_GROUP, S5_GROUP)
        cb_ref[j] = jnp.where(keep, replicate(cbq_ref[j], gc), 0.0).astype(BF16)


def _s5_kernel(h_ref, hprev_ref, g_ref, lam_ref, bq_ref, cq_ref, cbq_ref, dskip_ref, wglu_ref, o_ref,
               state_sc, ynext_sc, z_sc, slab_in_sc, slab_out_sc, b2_ref, c2_ref, cb_ref, *,
               tl, n_chunks, batch):
    @pl.when(pl.program_id(0) == 0)
    def _():
        state_sc[...] = jnp.zeros(state_sc.shape, F32)
        ynext_sc[...] = jnp.zeros(ynext_sc.shape, F32)
        z_sc[...] = jnp.zeros(z_sc.shape, BF16)
        _s5_pack_operators(bq_ref, cq_ref, cbq_ref, b2_ref, c2_ref, cb_ref, n_chunks)

    ns = S5_CHUNK_STATES
    d = h_ref.shape[-1]
    half = tl // 2
    glu_cols = 2 * d // n_chunks
    z_prev = z_sc[...]
    hn = _to_time_major(_rms(h_ref[...].reshape(batch * tl, d), g_ref[...]), slab_in_sc, tl, batch)
    hn = hn.reshape(half, 2 * batch, d)
    u_e = hn[:, :batch].reshape(half * batch, d)
    u_o = hn[:, batch:].reshape(half * batch, d)
    ue_b, uo_b = u_e.astype(BF16), u_o.astype(BF16)
    def input_proj(j):
        cols = slice(j * S5_CHUNK, (j + 1) * S5_CHUNK)
        return jnp.dot(jnp.concatenate([uo_b[:, cols], ue_b[:, cols]], axis=-1), b2_ref[j],
                       preferred_element_type=F32)

    z_e, z_o, zz = [], [], []
    bu_queue = [input_proj(0), input_proj(1)]
    for j in range(n_chunks):
        cols = slice(j * S5_CHUNK, (j + 1) * S5_CHUNK)
        bu = bu_queue.pop(0)
        if j + 2 < n_chunks:
            bu_queue.append(input_proj(j + 2))
        zz.append(jnp.dot(z_prev, wglu_ref[:, j * glu_cols:(j + 1) * glu_cols], preferred_element_type=F32))
        lam = lam_ref[j]
        lr, li = lam[:, :ns], lam[:, ns:]
        x = state_sc[j]
        xs = []
        for m in range(half):
            bu_m = bu[m * batch:(m + 1) * batch]
            xr, xi = x[:, :ns], x[:, ns:]
            nr = lr * xr - li * xi + bu_m[:, :ns]
            ni = lr * xi + li * xr + bu_m[:, ns:]
            x = jnp.concatenate([nr, ni], axis=-1)
            xs.append(x)
        state_sc[j] = x
        y2 = jnp.dot(jnp.concatenate(xs, axis=0).astype(BF16), c2_ref[j], preferred_element_type=F32)
        y_o, y_next = y2[:, :S5_CHUNK], y2[:, S5_CHUNK:]
        y_e = jnp.concatenate([ynext_sc[j], y_next[:-batch]], axis=0)
        ynext_sc[j] = y_next[-batch:]
        y_e = y_e + jnp.dot(ue_b[:, cols], cb_ref[j], preferred_element_type=F32)
        z_e.append(jax.nn.gelu(y_e + dskip_ref[:, cols] * u_e[:, cols]).astype(BF16))
        z_o.append(jax.nn.gelu(y_o + dskip_ref[:, cols] * u_o[:, cols]).astype(BF16))
    z_sc[...] = jnp.concatenate([jnp.concatenate(z_e, axis=-1), jnp.concatenate(z_o, axis=-1)], axis=0)
    zz = jnp.concatenate(zz, axis=-1)
    mixed = zz[:, :d] * jax.nn.sigmoid(zz[:, d:])
    mixed = jnp.concatenate([mixed[:half * batch].reshape(half, batch, d),
                             mixed[half * batch:].reshape(half, batch, d)], axis=1).reshape(tl * batch, d)
    mixed = _to_batch_major(mixed, slab_out_sc, tl, batch)
    o_ref[...] = hprev_ref[...] + mixed.reshape(batch, tl, d)


def _s5_discretize(a_re, a_im, b_re, b_im, c_re, c_im, log_step, batch):
    n, g, p = a_re.shape
    gc = S5_CHUNK // S5_GROUP
    n_chunks = g // gc
    dt = jnp.exp(log_step)[..., None]
    ar = jnp.minimum(a_re, -1e-4)
    ai = a_im
    mag = jnp.exp(ar * dt)
    lr = mag * jnp.cos(ai * dt)
    li = mag * jnp.sin(ai * dt)
    nr = lr - 1.0
    den = ar * ar + ai * ai
    fr = (nr * ar + li * ai) / den
    fi = (li * ar - nr * ai) / den
    bbr = fr[..., None] * b_re - fi[..., None] * b_im
    bbi = fr[..., None] * b_im + fi[..., None] * b_re
    lbr = lr[..., None] * bbr - li[..., None] * bbi
    lbi = lr[..., None] * bbi + li[..., None] * bbr
    clr = c_re * lr[:, :, None, :] - c_im * li[:, :, None, :]
    cli = c_re * li[:, :, None, :] + c_im * lr[:, :, None, :]
    cb = jnp.einsum('ngjp,ngpk->ngkj', c_re, bbr) - jnp.einsum('ngjp,ngpk->ngkj', c_im, bbi)
    bq = jnp.stack([bbr, bbi, lbr, lbi]).reshape(4, n, n_chunks, gc * p, S5_GROUP)
    cq = jnp.stack([c_re, -c_im, clr, -cli]).reshape(4, n, n_chunks, S5_CHUNK, p)
    cbq = cb.reshape(n, n_chunks, S5_CHUNK, S5_GROUP)
    lam2 = jnp.concatenate([(lr * lr - li * li).reshape(n, n_chunks, 1, gc * p),
                            (2.0 * lr * li).reshape(n, n_chunks, 1, gc * p)], axis=-1)
    lam2 = jnp.broadcast_to(lam2, (n, n_chunks, batch, 2 * gc * p))
    return lam2, bq, cq, cbq


def _s5_mixer(h, g, s5_params, d_skip, w_glu, layer, *, batch, tl=64):
    t, d = h.shape
    seq = t // batch
    n_chunks = d // S5_CHUNK
    ns2 = 2 * S5_CHUNK_STATES
    rows = tl * batch
    nt = seq // tl
    assert batch == 8 and tl % 2 == 0 and seq % tl == 0 and d % S5_CHUNK == 0
    h3 = h.reshape(batch, seq, d)
    out = pl.pallas_call(
        functools.partial(_s5_kernel, tl=tl, n_chunks=n_chunks, batch=batch),
        out_shape=jax.ShapeDtypeStruct((batch, seq, d), F32),
        grid=(nt + 1,),
        in_specs=[
            pl.BlockSpec((batch, tl, d), lambda i: (0, jnp.minimum(i, nt - 1), 0)),
            pl.BlockSpec((batch, tl, d), lambda i: (0, jnp.maximum(i - 1, 0), 0)),
            _const_spec((1, d)),
            _const_spec((None, n_chunks, batch, ns2), (layer, 0, 0, 0)),
            _const_spec((4, None, n_chunks, S5_CHUNK_STATES, S5_GROUP), (0, layer, 0, 0, 0)),
            _const_spec((4, None, n_chunks, S5_CHUNK, S5_STATE), (0, layer, 0, 0, 0)),
            _const_spec((None, n_chunks, S5_CHUNK, S5_GROUP), (layer, 0, 0, 0)),
            _const_spec((1, d)),
            _const_spec((None, d, 2 * d), (layer, 0, 0)),
        ],
        out_specs=pl.BlockSpec((batch, tl, d), lambda i: (0, jnp.maximum(i - 1, 0), 0)),
        scratch_shapes=[
            pltpu.VMEM((n_chunks, batch, ns2), F32),
            pltpu.VMEM((n_chunks, batch, S5_CHUNK), F32),
            pltpu.VMEM((rows, d), BF16),
            pltpu.VMEM((d // LANES, rows, LANES), F32),
            pltpu.VMEM((d // LANES, rows, LANES), F32),
            pltpu.VMEM((n_chunks, 2 * S5_CHUNK, ns2), BF16),
            pltpu.VMEM((n_chunks, ns2, 2 * S5_CHUNK), BF16),
            pltpu.VMEM((n_chunks, S5_CHUNK, S5_CHUNK), BF16),
        ],
        compiler_params=_params(("arbitrary",)),
        name="s5_mixer",
    )(h3, h3, g.reshape(1, d), *s5_params, d_skip.reshape(1, d), w_glu)
    return out.reshape(t, d)


def _kv_kernel(x_ref, g_ref, wk_ref, wvt_ref, k_ref, vt_ref, km_ref, *, ck):
    xn = _rms(x_ref[0], g_ref[...]).astype(BF16)
    seq, d = xn.shape
    nb = seq // MOBA_BLOCK
    for c in range(d // ck):
        cols = slice(c * ck, (c + 1) * ck)
        kc = jnp.dot(xn, wk_ref[:, cols], preferred_element_type=F32)
        k_ref[0, :, cols] = kc.astype(BF16)
        km_ref[0, :, cols] = jnp.mean(kc.reshape(nb, MOBA_BLOCK, ck), axis=1)
        vt_ref[0, cols, :] = lax.dot_general(wvt_ref[cols, :], xn, NT,
                                             preferred_element_type=F32).astype(BF16)


def _kv_proj(h, g, w_kv, *, batch, ck=256):
    t, d = h.shape
    seq = t // batch
    nb = seq // MOBA_BLOCK
    w_k = w_kv[:, :d].astype(BF16)
    w_vt = w_kv[:, d:].T.astype(BF16)
    return pl.pallas_call(
        functools.partial(_kv_kernel, ck=ck),
        out_shape=(jax.ShapeDtypeStruct((batch, seq, d), BF16),
                   jax.ShapeDtypeStruct((batch, d, seq), BF16),
                   jax.ShapeDtypeStruct((batch, nb, d), F32)),
        grid=(batch,),
        in_specs=[
            pl.BlockSpec((1, seq, d), lambda b: (b, 0, 0)),
            _const_spec((1, d)),
            _const_spec((d, d)),
            _const_spec((d, d)),
        ],
        out_specs=(pl.BlockSpec((1, seq, d), lambda b: (b, 0, 0)),
                   pl.BlockSpec((1, d, seq), lambda b: (b, 0, 0)),
                   pl.BlockSpec((1, nb, d), lambda b: (b, 0, 0))),
        compiler_params=_params(("arbitrary",)),
        name="kv_proj",
    )(h.reshape(batch, seq, d), g.reshape(1, d), w_k, w_vt)


def _moba_kernel(slopes_ref, q_ref, k_ref, vt_ref, km_ref, o_ref, ot_sc, t_sc, p_sc, *,
                 nb, head_dim, pairs):
    bs = MOBA_BLOCK
    pair = pl.program_id(0) % pairs
    heads = LANES // head_dim
    seq = q_ref.shape[1]
    q = q_ref[0].reshape(nb, bs, LANES)
    k = k_ref[0].reshape(nb, bs, LANES)
    km = km_ref[0].astype(BF16)
    lane = lax.broadcasted_iota(jnp.int32, (1, LANES), 1)
    pos = lax.broadcasted_iota(jnp.int32, (bs, LANES), 0).astype(F32)
    causal = (lax.broadcasted_iota(jnp.int32, (bs, bs), 1)
              >= lax.broadcasted_iota(jnp.int32, (bs, bs), 0))
    blk = lax.broadcasted_iota(jnp.int32, (nb, bs), 0)

    def split3(x):
        hi = x.astype(BF16).astype(F32)
        mid = (x - hi).astype(BF16).astype(F32)
        lo = (x - hi - mid).astype(BF16).astype(F32)
        return hi, mid, lo

    lane_b = lax.broadcasted_iota(jnp.int32, (bs, LANES), 1)

    def lanes_of(terms, base):
        out = jnp.zeros((bs, LANES), F32)
        for j, term in enumerate(terms):
            out = jnp.where(lane_b == base + j, term, out)
        return out.astype(BF16)

    slope, gate_t, q_aug, k_aug = [], [], [], []
    for hh in range(heads):
        slope.append(slopes_ref[pair * heads + hh])
        in_head = lane // head_dim == hh
        fb = ((hh + 1) % heads) * head_dim
        slope_v = jnp.full((bs, LANES), slope[hh], F32)
        q_feat = lanes_of(split3(slope_v) + split3(-slope_v * pos), fb)
        k_feat = lanes_of([pos] * 3 + [jnp.ones((bs, LANES), F32)] * 3, fb)
        q_aug.append(jnp.where(in_head, q, q_feat[None]).reshape(seq, LANES))
        k_aug.append(jnp.where(in_head, k, k_feat[None]).reshape(seq, LANES))
        gate_t.append(lax.dot_general(jnp.where(in_head, km, jnp.zeros_like(km)), q_aug[hh], NT,
                                      preferred_element_type=F32))

    def scores(i, hh, slot):
        s = lax.dot_general(k_aug[hh][0:(i + 1) * bs], q_aug[hh][i * bs:(i + 1) * bs], NT,
                            preferred_element_type=F32)
        col_max = []
        for n in range(i + 1):
            t_n = s[n * bs:(n + 1) * bs]
            if n == i:
                t_n = jnp.where(causal, t_n, NEG)
            t_sc[slot, n * bs:(n + 1) * bs, :] = t_n
            col_max.append(jnp.max(t_n, axis=0, keepdims=True))
        return col_max

    def finish(i, hh, slot, p_slot, col_max):
        if i > MOBA_TOPK:
            g = gate_t[hh][:, i * bs:(i + 1) * bs]
            rank = jnp.zeros((nb, bs), jnp.int32)
            for m in range(i):
                gm = g[m:m + 1, :]
                rank = rank + ((gm > g) | ((gm == g) & (m < blk))).astype(jnp.int32)
            sel = rank < MOBA_TOPK
        bias = []
        for n in range(i):
            b_n = -slope[hh] * float((i - n) * bs)
            if i > MOBA_TOPK:
                b_n = jnp.where(sel[n:n + 1, :], b_n, NEG)
            bias.append(b_n)
        bias.append(0.0)
        m_row = col_max[i]
        for n in range(i):
            m_row = jnp.maximum(m_row, col_max[n] + bias[n])
        l_row = jnp.zeros((1, bs), F32)
        for n in range(i + 1):
            p_n = jnp.exp2(t_sc[slot, n * bs:(n + 1) * bs, :] - (m_row - bias[n]))
            l_row = l_row + jnp.sum(p_n, axis=0, keepdims=True)
            p_sc[p_slot, n * bs:(n + 1) * bs, :] = p_n.astype(BF16)
        o_t = jnp.dot(vt_ref[0, hh * head_dim:(hh + 1) * head_dim, 0:(i + 1) * bs],
                      p_sc[p_slot, 0:(i + 1) * bs, :], preferred_element_type=F32)
        ot_sc[hh * head_dim:(hh + 1) * head_dim, i * bs:(i + 1) * bs] = o_t / l_row
        if hh == heads - 1:
            o_ref[0, i * bs:(i + 1) * bs, :] = ot_sc[:, i * bs:(i + 1) * bs].T.astype(BF16)

    items = [(i, hh) for i in reversed(range(nb)) for hh in range(heads)]
    ahead = t_sc.shape[0] - 1
    pending = [scores(*item, slot) for slot, item in enumerate(items[:ahead])]
    for idx, item in enumerate(items):
        if idx + ahead < len(items):
            pending.append(scores(*items[idx + ahead], (idx + ahead) % (ahead + 1)))
        finish(*item, idx % (ahead + 1), idx % p_sc.shape[0], pending.pop(0))


def _moba(q, k, vt, kmean, *, batch, seq, d):
    nb = seq // MOBA_BLOCK
    pairs = d // LANES
    head_dim = d // N_HEADS
    assert seq % MOBA_BLOCK == 0 and d % LANES == 0 and LANES // head_dim == 2
    slopes = LOG2E * jnp.exp2(-8.0 * jnp.arange(1, N_HEADS + 1, dtype=F32) / N_HEADS)
    grid_spec = pltpu.PrefetchScalarGridSpec(
        num_scalar_prefetch=1,
        grid=(batch * pairs,),
        in_specs=[
            pl.BlockSpec((1, seq, LANES), lambda bp, s: (bp // pairs, 0, bp % pairs)),
            pl.BlockSpec((1, seq, LANES), lambda bp, s: (bp // pairs, 0, bp % pairs)),
            pl.BlockSpec((1, LANES, seq), lambda bp, s: (bp // pairs, bp % pairs, 0)),
            pl.BlockSpec((1, nb, LANES), lambda bp, s: (bp // pairs, 0, bp % pairs)),
        ],
        out_specs=pl.BlockSpec((1, seq, LANES), lambda bp, s: (bp // pairs, 0, bp % pairs)),
        scratch_shapes=[
            pltpu.VMEM((LANES, seq), F32),
            pltpu.VMEM((MOBA_AHEAD + 1, seq, MOBA_BLOCK), F32),
            pltpu.VMEM((2, seq, MOBA_BLOCK), BF16),
        ],
    )
    o = pl.pallas_call(
        functools.partial(_moba_kernel, nb=nb, head_dim=head_dim, pairs=pairs),
        out_shape=jax.ShapeDtypeStruct((batch, seq, d), BF16),
        grid_spec=grid_spec,
        compiler_params=_params(("arbitrary",)),
        name="moba",
    )(slopes, q.reshape(batch, seq, d), k, vt, kmean)
    return o.reshape(batch * seq, d)


def kernel(x, ffn1_norm, ffn1_w_in, ffn1_w_out, mix_norm, ffn2_norm, ffn2_w_in, ffn2_w_out,
           s5_a_re, s5_a_im, s5_b_re, s5_b_im, s5_c_re, s5_c_im, s5_d, s5_log_step, s5_w_glu,
           kv_norm, w_kv, w_q, w_o, final_norm):
    batch, seq, d = x.shape
    depth = ffn1_norm.shape[0]
    n_a = s5_a_re.shape[0]
    h = x.reshape(batch * seq, d)
    s5_params = _s5_discretize(s5_a_re, s5_a_im, s5_b_re, s5_b_im, s5_c_re, s5_c_im, s5_log_step, batch)
    w_glu = s5_w_glu.astype(BF16)
    k = vt = kmean = None
    for layer in range(depth):
        if layer == n_a:
            k, vt, kmean = _kv_proj(h, kv_norm, w_kv, batch=batch)
        if layer < n_a:
            h = _ffn(h, ffn1_norm[layer], ffn1_w_in, ffn1_w_out, layer)
            h = _s5_mixer(h, mix_norm[layer], s5_params, s5_d[layer], w_glu, layer, batch=batch)
            attn = None
        else:
            j = layer - n_a
            h, q = _ffn(h, ffn1_norm[layer], ffn1_w_in, ffn1_w_out, layer,
                        q_proj=(mix_norm[layer], w_q[j], LOG2E * (d // N_HEADS) ** -0.5))
            attn = (_moba(q, k, vt, kmean, batch=batch, seq=seq, d=d), w_o[j])
        last = layer == depth - 1
        h = _ffn(h, ffn2_norm[layer], ffn2_w_in, ffn2_w_out, layer, attn=attn, final_g=final_norm if last else None)
    return h.reshape(batch, seq, d)
```

```python
import functools

import jax
import jax.numpy as jnp
from jax import lax
from jax.experimental import pallas as pl
from jax.experimental.pallas import tpu as pltpu

S5_GROUP = 16
S5_STATE = 64
N_HEADS = 16
MOBA_BLOCK = 256
MOBA_TOPK = 3
EPS = 1e-6
NEG = -1e30
LOG2E = 1.4426950408889634
MOBA_AHEAD = 3

LANES = 128
S5_CHUNK = LANES
S5_CHUNK_STATES = S5_CHUNK // S5_GROUP * S5_STATE
VMEM_LIMIT = 56 * 1024 * 1024

BF16 = jnp.bfloat16
F32 = jnp.float32
NT = (((1,), (1,)), ((), ()))


def _rms(x, g):
    return x * lax.rsqrt(jnp.mean(x * x, axis=-1, keepdims=True) + EPS) * g


def _const_spec(shape, index=None):
    index = (0,) * len(shape) if index is None else index
    return pl.BlockSpec(shape, lambda *_: index, pipeline_mode=pl.Buffered(1))


def _params(semantics):
    return pltpu.CompilerParams(dimension_semantics=semantics, vmem_limit_bytes=VMEM_LIMIT)


def _ffn_kernel(*refs, d_ff, ck, n_w, attn_in, q_out, final_norm, q_scale):
    refs = list(refs)
    x_ref = refs.pop(0)
    attn_ref, wo_ref = (refs.pop(0), refs.pop(0)) if attn_in else (None, None)
    g_ref, win_ref, wout_ref = refs.pop(0), refs.pop(0), refs.pop(0)
    gq_ref, wq_ref = (refs.pop(0), refs.pop(0)) if q_out else (None, None)
    fg_ref = refs.pop(0) if final_norm else None
    o_ref = refs.pop(0)
    q_ref = refs.pop(0) if q_out else None
    win_sc, wout_sc = refs
    step = pl.program_id(0)

    @pl.when(step < n_w)
    def _():
        r_in, r_out = win_ref.shape[0], wout_ref.shape[0]
        win_sc[pl.ds(pl.multiple_of(step * r_in, r_in), r_in), :] = win_ref[...].astype(BF16)
        wout_sc[pl.ds(pl.multiple_of(step * r_out, r_out), r_out), :] = wout_ref[...].astype(BF16)

    @pl.when(step >= n_w)
    def _():
        x = x_ref[...]
        if attn_in:
            x = x + jnp.dot(attn_ref[...], wo_ref[...], preferred_element_type=F32)
        xn = _rms(x, g_ref[...]).astype(BF16)
        acc = jnp.zeros(x.shape, F32)
        for c in range(d_ff // ck):
            gate = jnp.dot(xn, win_sc[:, c * ck:(c + 1) * ck], preferred_element_type=F32)
            up = jnp.dot(xn, win_sc[:, d_ff + c * ck:d_ff + (c + 1) * ck], preferred_element_type=F32)
            a = (gate * jax.nn.sigmoid(gate) * up).astype(BF16)
            acc = acc + jnp.dot(a, wout_sc[c * ck:(c + 1) * ck, :], preferred_element_type=F32)
        y = x + 0.5 * acc
        if q_out:
            yn = _rms(y, gq_ref[...]).astype(BF16)
            q_ref[...] = (jnp.dot(yn, wq_ref[...], preferred_element_type=F32) * q_scale).astype(BF16)
        if final_norm:
            y = _rms(y, fg_ref[...])
        o_ref[...] = y


def _ffn(h, g, w_in, w_out, layer, *, attn=None, q_proj=None, final_g=None, tm=1024, ck=256, n_w=16):
    t, d = h.shape
    d_ff = w_out.shape[1]
    bf16_rows = 16
    assert t % tm == 0 and d_ff % ck == 0 and d % (n_w * bf16_rows) == 0 and d_ff % (n_w * bf16_rows) == 0
    row_spec = pl.BlockSpec((tm, d), lambda i: (jnp.maximum(i - n_w, 0), 0))
    slab = lambda i: (layer, jnp.minimum(i, n_w - 1), 0)
    args, in_specs = [h], [row_spec]
    if attn is not None:
        args += [attn[0], attn[1].astype(BF16)]
        in_specs += [row_spec, _const_spec((d, d))]
    args += [g.reshape(1, d), w_in, w_out]
    in_specs += [_const_spec((1, d)), pl.BlockSpec((None, d // n_w, 2 * d_ff), slab),
                 pl.BlockSpec((None, d_ff // n_w, d), slab)]
    out_shape, out_specs = [jax.ShapeDtypeStruct((t, d), F32)], [row_spec]
    if q_proj is not None:
        args += [q_proj[0].reshape(1, d), q_proj[1].astype(BF16)]
        in_specs += [_const_spec((1, d)), _const_spec((d, d))]
        out_shape.append(jax.ShapeDtypeStruct((t, d), BF16))
        out_specs.append(row_spec)
    if final_g is not None:
        args.append(final_g.reshape(1, d))
        in_specs.append(_const_spec((1, d)))
    out = pl.pallas_call(
        functools.partial(_ffn_kernel, d_ff=d_ff, ck=ck, n_w=n_w, attn_in=attn is not None,
                          q_out=q_proj is not None, final_norm=final_g is not None,
                          q_scale=q_proj[2] if q_proj else None),
        out_shape=out_shape,
        grid=(n_w + t // tm,),
        in_specs=in_specs,
        out_specs=out_specs,
        scratch_shapes=[pltpu.VMEM((d, 2 * d_ff), BF16), pltpu.VMEM((d_ff, d), BF16)],
        compiler_params=_params(("arbitrary",)),
        name="ffn",
    )(*args)
    return out if q_proj is not None else out[0]


def _to_time_major(x, slab_sc, tl, batch):
    slabs = x.shape[1] // LANES
    for s in range(slabs):
        for b in range(batch):
            slab_sc[s, pl.ds(b, tl, stride=batch), :] = x[b * tl:(b + 1) * tl, s * LANES:(s + 1) * LANES]
    return jnp.concatenate([slab_sc[s] for s in range(slabs)], axis=-1)


def _to_batch_major(y, slab_sc, tl, batch):
    slabs = y.shape[1] // LANES
    for s in range(slabs):
        slab_sc[s] = y[:, s * LANES:(s + 1) * LANES]
    return jnp.concatenate(
        [jnp.concatenate([slab_sc[s, pl.ds(b, tl, stride=batch), :] for s in range(slabs)], axis=-1)
         for b in range(batch)], axis=0)


def _s5_pack_operators(bq_ref, cq_ref, cbq_ref, b2_ref, c2_ref, cb_ref, n_chunks):
    def replicate(x, reps):
        w = x.shape[1]
        sel = (lax.broadcasted_iota(jnp.int32, (w, w * reps), 1) % w
               == lax.broadcasted_iota(jnp.int32, (w, w * reps), 0)).astype(BF16)
        return jnp.dot(x.astype(BF16), sel, preferred_element_type=F32)

    def own_group(shape, rows_per_group, lanes_per_group):
        return (lax.broadcasted_iota(jnp.int32, shape, 0) // rows_per_group
                == lax.broadcasted_iota(jnp.int32, shape, 1) // lanes_per_group)

    gc = S5_CHUNK // S5_GROUP
    for j in range(n_chunks):
        keep = own_group((gc * S5_STATE, S5_CHUNK), S5_STATE, S5_GROUP)
        b2t = jnp.concatenate(
            [jnp.concatenate([jnp.where(keep, replicate(bq_ref[2 * r + q, j], gc), 0.0) for r in range(2)], axis=1)
             for q in range(2)], axis=0)
        b2_ref[j] = b2t.T.astype(BF16)
        keep = own_group((S5_CHUNK, gc * S5_STATE), S5_GROUP, S5_STATE)
        c2t = jnp.concatenate(
            [jnp.concatenate([jnp.where(keep, replicate(cq_ref[2 * s + q, j], gc), 0.0) for q in range(2)], axis=1)
             for s in range(2)], axis=0)
        c2_ref[j] = c2t.T.astype(BF16)
        keep = own_group((S5_CHUNK, S5_CHUNK), S5_GROUP, S5_GROUP)
        cb_ref[j] = jnp.where(keep, replicate(cbq_ref[j], gc), 0.0).astype(BF16)


def _s5_kernel(h_ref, hprev_ref, g_ref, lam_ref, bq_ref, cq_ref, cbq_ref, dskip_ref, wglu_ref, o_ref,
               state_sc, ynext_sc, z_sc, slab_in_sc, slab_out_sc, b2_ref, c2_ref, cb_ref, *,
               tl, n_chunks, batch):
    @pl.when(pl.program_id(0) == 0)
    def _():
        state_sc[...] = jnp.zeros(state_sc.shape, F32)
        ynext_sc[...] = jnp.zeros(ynext_sc.shape, F32)
        z_sc[...] = jnp.zeros(z_sc.shape, BF16)
        _s5_pack_operators(bq_ref, cq_ref, cbq_ref, b2_ref, c2_ref, cb_ref, n_chunks)

    ns = S5_CHUNK_STATES
    d = h_ref.shape[-1]
    half = tl // 2
    glu_cols = 2 * d // n_chunks
    z_prev = z_sc[...]
    hn = _to_time_major(_rms(h_ref[...].reshape(batch * tl, d), g_ref[...]), slab_in_sc, tl, batch)
    hn = hn.reshape(half, 2 * batch, d)
    u_e = hn[:, :batch].reshape(half * batch, d)
    u_o = hn[:, batch:].reshape(half * batch, d)
    ue_b, uo_b = u_e.astype(BF16), u_o.astype(BF16)
    def input_proj(j):
        cols = slice(j * S5_CHUNK, (j + 1) * S5_CHUNK)
        return jnp.dot(jnp.concatenate([uo_b[:, cols], ue_b[:, cols]], axis=-1), b2_ref[j],
                       preferred_element_type=F32)

    z_e, z_o, zz = [], [], []
    bu_next = input_proj(0)
    for j in range(n_chunks):
        cols = slice(j * S5_CHUNK, (j + 1) * S5_CHUNK)
        bu = bu_next
        if j + 1 < n_chunks:
            bu_next = input_proj(j + 1)
        zz.append(jnp.dot(z_prev, wglu_ref[:, j * glu_cols:(j + 1) * glu_cols], preferred_element_type=F32))
        lam = lam_ref[j]
        lr, li = lam[:, :ns], lam[:, ns:]
        x = state_sc[j]
        xs = []
        for m in range(half):
            bu_m = bu[m * batch:(m + 1) * batch]
            xr, xi = x[:, :ns], x[:, ns:]
            nr = lr * xr - li * xi + bu_m[:, :ns]
            ni = lr * xi + li * xr + bu_m[:, ns:]
            x = jnp.concatenate([nr, ni], axis=-1)
            xs.append(x)
        state_sc[j] = x
        y2 = jnp.dot(jnp.concatenate(xs, axis=0).astype(BF16), c2_ref[j], preferred_element_type=F32)
        y_o, y_next = y2[:, :S5_CHUNK], y2[:, S5_CHUNK:]
        y_e = jnp.concatenate([ynext_sc[j], y_next[:-batch]], axis=0)
        ynext_sc[j] = y_next[-batch:]
        y_e = y_e + jnp.dot(ue_b[:, cols], cb_ref[j], preferred_element_type=F32)
        z_e.append(jax.nn.gelu(y_e + dskip_ref[:, cols] * u_e[:, cols]).astype(BF16))
        z_o.append(jax.nn.gelu(y_o + dskip_ref[:, cols] * u_o[:, cols]).astype(BF16))
    z_sc[...] = jnp.concatenate([jnp.concatenate(z_e, axis=-1), jnp.concatenate(z_o, axis=-1)], axis=0)
    zz = jnp.concatenate(zz, axis=-1)
    mixed = zz[:, :d] * jax.nn.sigmoid(zz[:, d:])
    mixed = jnp.concatenate([mixed[:half * batch].reshape(half, batch, d),
                             mixed[half * batch:].reshape(half, batch, d)], axis=1).reshape(tl * batch, d)
    mixed = _to_batch_major(mixed, slab_out_sc, tl, batch)
    o_ref[...] = hprev_ref[...] + mixed.reshape(batch, tl, d)


def _s5_discretize(a_re, a_im, b_re, b_im, c_re, c_im, log_step, batch):
    n, g, p = a_re.shape
    gc = S5_CHUNK // S5_GROUP
    n_chunks = g // gc
    dt = jnp.exp(log_step)[..., None]
    ar = jnp.minimum(a_re, -1e-4)
    ai = a_im
    mag = jnp.exp(ar * dt)
    lr = mag * jnp.cos(ai * dt)
    li = mag * jnp.sin(ai * dt)
    nr = lr - 1.0
    den = ar * ar + ai * ai
    fr = (nr * ar + li * ai) / den
    fi = (li * ar - nr * ai) / den
    bbr = fr[..., None] * b_re - fi[..., None] * b_im
    bbi = fr[..., None] * b_im + fi[..., None] * b_re
    lbr = lr[..., None] * bbr - li[..., None] * bbi
    lbi = lr[..., None] * bbi + li[..., None] * bbr
    clr = c_re * lr[:, :, None, :] - c_im * li[:, :, None, :]
    cli = c_re * li[:, :, None, :] + c_im * lr[:, :, None, :]
    cb = jnp.einsum('ngjp,ngpk->ngkj', c_re, bbr) - jnp.einsum('ngjp,ngpk->ngkj', c_im, bbi)
    bq = jnp.stack([bbr, bbi, lbr, lbi]).reshape(4, n, n_chunks, gc * p, S5_GROUP)
    cq = jnp.stack([c_re, -c_im, clr, -cli]).reshape(4, n, n_chunks, S5_CHUNK, p)
    cbq = cb.reshape(n, n_chunks, S5_CHUNK, S5_GROUP)
    lam2 = jnp.concatenate([(lr * lr - li * li).reshape(n, n_chunks, 1, gc * p),
                            (2.0 * lr * li).reshape(n, n_chunks, 1, gc * p)], axis=-1)
    lam2 = jnp.broadcast_to(lam2, (n, n_chunks, batch, 2 * gc * p))
    return lam2, bq, cq, cbq


def _s5_mixer(h, g, s5_params, d_skip, w_glu, layer, *, batch, tl=64):
    t, d = h.shape
    seq = t // batch
    n_chunks = d // S5_CHUNK
    ns2 = 2 * S5_CHUNK_STATES
    rows = tl * batch
    nt = seq // tl
    assert batch == 8 and tl % 2 == 0 and seq % tl == 0 and d % S5_CHUNK == 0
    h3 = h.reshape(batch, seq, d)
    out = pl.pallas_call(
        functools.partial(_s5_kernel, tl=tl, n_chunks=n_chunks, batch=batch),
        out_shape=jax.ShapeDtypeStruct((batch, seq, d), F32),
        grid=(nt + 1,),
        in_specs=[
            pl.BlockSpec((batch, tl, d), lambda i: (0, jnp.minimum(i, nt - 1), 0)),
            pl.BlockSpec((batch, tl, d), lambda i: (0, jnp.maximum(i - 1, 0), 0)),
            _const_spec((1, d)),
            _const_spec((None, n_chunks, batch, ns2), (layer, 0, 0, 0)),
            _const_spec((4, None, n_chunks, S5_CHUNK_STATES, S5_GROUP), (0, layer, 0, 0, 0)),
            _const_spec((4, None, n_chunks, S5_CHUNK, S5_STATE), (0, layer, 0, 0, 0)),
            _const_spec((None, n_chunks, S5_CHUNK, S5_GROUP), (layer, 0, 0, 0)),
            _const_spec((1, d)),
            _const_spec((None, d, 2 * d), (layer, 0, 0)),
        ],
        out_specs=pl.BlockSpec((batch, tl, d), lambda i: (0, jnp.maximum(i - 1, 0), 0)),
        scratch_shapes=[
            pltpu.VMEM((n_chunks, batch, ns2), F32),
            pltpu.VMEM((n_chunks, batch, S5_CHUNK), F32),
            pltpu.VMEM((rows, d), BF16),
            pltpu.VMEM((d // LANES, rows, LANES), F32),
            pltpu.VMEM((d // LANES, rows, LANES), F32),
            pltpu.VMEM((n_chunks, 2 * S5_CHUNK, ns2), BF16),
            pltpu.VMEM((n_chunks, ns2, 2 * S5_CHUNK), BF16),
            pltpu.VMEM((n_chunks, S5_CHUNK, S5_CHUNK), BF16),
        ],
        compiler_params=_params(("arbitrary",)),
        name="s5_mixer",
    )(h3, h3, g.reshape(1, d), *s5_params, d_skip.reshape(1, d), w_glu)
    return out.reshape(t, d)


def _kv_kernel(x_ref, g_ref, wk_ref, wvt_ref, k_ref, vt_ref, km_ref, *, ck):
    xn = _rms(x_ref[0], g_ref[...]).astype(BF16)
    seq, d = xn.shape
    nb = seq // MOBA_BLOCK
    for c in range(d // ck):
        cols = slice(c * ck, (c + 1) * ck)
        kc = jnp.dot(xn, wk_ref[:, cols], preferred_element_type=F32)
        k_ref[0, :, cols] = kc.astype(BF16)
        km_ref[0, :, cols] = jnp.mean(kc.reshape(nb, MOBA_BLOCK, ck), axis=1)
        vt_ref[0, cols, :] = lax.dot_general(wvt_ref[cols, :], xn, NT,
                                             preferred_element_type=F32).astype(BF16)


def _kv_proj(h, g, w_kv, *, batch, ck=256):
    t, d = h.shape
    seq = t // batch
    nb = seq // MOBA_BLOCK
    w_k = w_kv[:, :d].astype(BF16)
    w_vt = w_kv[:, d:].T.astype(BF16)
    return pl.pallas_call(
        functools.partial(_kv_kernel, ck=ck),
        out_shape=(jax.ShapeDtypeStruct((batch, seq, d), BF16),
                   jax.ShapeDtypeStruct((batch, d, seq), BF16),
                   jax.ShapeDtypeStruct((batch, nb, d), F32)),
        grid=(batch,),
        in_specs=[
            pl.BlockSpec((1, seq, d), lambda b: (b, 0, 0)),
            _const_spec((1, d)),
            _const_spec((d, d)),
            _const_spec((d, d)),
        ],
        out_specs=(pl.BlockSpec((1, seq, d), lambda b: (b, 0, 0)),
                   pl.BlockSpec((1, d, seq), lambda b: (b, 0, 0)),
                   pl.BlockSpec((1, nb, d), lambda b: (b, 0, 0))),
        compiler_params=_params(("arbitrary",)),
        name="kv_proj",
    )(h.reshape(batch, seq, d), g.reshape(1, d), w_k, w_vt)


def _moba_kernel(slopes_ref, q_ref, k_ref, vt_ref, km_ref, o_ref, ot_sc, t_sc, p_sc, *,
                 nb, head_dim, pairs):
    bs = MOBA_BLOCK
    pair = pl.program_id(0) % pairs
    heads = LANES // head_dim
    seq = q_ref.shape[1]
    q = q_ref[0].reshape(nb, bs, LANES)
    k = k_ref[0].reshape(nb, bs, LANES)
    km = km_ref[0].astype(BF16)
    lane = lax.broadcasted_iota(jnp.int32, (1, LANES), 1)
    pos = lax.broadcasted_iota(jnp.int32, (bs, LANES), 0).astype(F32)
    causal = (lax.broadcasted_iota(jnp.int32, (bs, bs), 1)
              >= lax.broadcasted_iota(jnp.int32, (bs, bs), 0))
    blk = lax.broadcasted_iota(jnp.int32, (nb, bs), 0)

    def split3(x):
        hi = x.astype(BF16).astype(F32)
        mid = (x - hi).astype(BF16).astype(F32)
        lo = (x - hi - mid).astype(BF16).astype(F32)
        return hi, mid, lo

    lane_b = lax.broadcasted_iota(jnp.int32, (bs, LANES), 1)

    def lanes_of(terms, base):
        out = jnp.zeros((bs, LANES), F32)
        for j, term in enumerate(terms):
            out = jnp.where(lane_b == base + j, term, out)
        return out.astype(BF16)

    slope, gate_t, q_aug, k_aug = [], [], [], []
    for hh in range(heads):
        slope.append(slopes_ref[pair * heads + hh])
        in_head = lane // head_dim == hh
        fb = ((hh + 1) % heads) * head_dim
        slope_v = jnp.full((bs, LANES), slope[hh], F32)
        q_feat = lanes_of(split3(slope_v) + split3(-slope_v * pos), fb)
        k_feat = lanes_of([pos] * 3 + [jnp.ones((bs, LANES), F32)] * 3, fb)
        q_aug.append(jnp.where(in_head, q, q_feat[None]).reshape(seq, LANES))
        k_aug.append(jnp.where(in_head, k, k_feat[None]).reshape(seq, LANES))
        gate_t.append(lax.dot_general(jnp.where(in_head, km, jnp.zeros_like(km)), q_aug[hh], NT,
                                      preferred_element_type=F32))

    def scores(i, hh, slot):
        s = lax.dot_general(k_aug[hh][0:(i + 1) * bs], q_aug[hh][i * bs:(i + 1) * bs], NT,
                            preferred_element_type=F32)
        col_max = []
        for n in range(i + 1):
            t_n = s[n * bs:(n + 1) * bs]
            if n == i:
                t_n = jnp.where(causal, t_n, NEG)
            t_sc[slot, n * bs:(n + 1) * bs, :] = t_n
            col_max.append(jnp.max(t_n, axis=0, keepdims=True))
        return col_max

    def finish(i, hh, slot, p_slot, col_max):
        if i > MOBA_TOPK:
            g = gate_t[hh][:, i * bs:(i + 1) * bs]
            rank = jnp.zeros((nb, bs), jnp.int32)
            for m in range(i):
                gm = g[m:m + 1, :]
                rank = rank + ((gm > g) | ((gm == g) & (m < blk))).astype(jnp.int32)
            sel = rank < MOBA_TOPK
        bias = []
        for n in range(i):
            b_n = -slope[hh] * float((i - n) * bs)
            if i > MOBA_TOPK:
                b_n = jnp.where(sel[n:n + 1, :], b_n, NEG)
            bias.append(b_n)
        bias.append(0.0)
        m_row = col_max[i]
        for n in range(i):
            m_row = jnp.maximum(m_row, col_max[n] + bias[n])
        l_row = jnp.zeros((1, bs), F32)
        for n in range(i + 1):
            p_n = jnp.exp2(t_sc[slot, n * bs:(n + 1) * bs, :] - (m_row - bias[n]))
            l_row = l_row + jnp.sum(p_n, axis=0, keepdims=True)
            p_sc[p_slot, n * bs:(n + 1) * bs, :] = p_n.astype(BF16)
        o_t = jnp.dot(vt_ref[0, hh * head_dim:(hh + 1) * head_dim, 0:(i + 1) * bs],
                      p_sc[p_slot, 0:(i + 1) * bs, :], preferred_element_type=F32)
        ot_sc[hh * head_dim:(hh + 1) * head_dim, i * bs:(i + 1) * bs] = o_t / l_row
        if hh == heads - 1:
            o_ref[0, i * bs:(i + 1) * bs, :] = ot_sc[:, i * bs:(i + 1) * bs].T.astype(BF16)

    items = [(i, hh) for i in reversed(range(nb)) for hh in range(heads)]
    ahead = t_sc.shape[0] - 1
    pending = [scores(*item, slot) for slot, item in enumerate(items[:ahead])]
    for idx, item in enumerate(items):
        if idx + ahead < len(items):
            pending.append(scores(*items[idx + ahead], (idx + ahead) % (ahead + 1)))
        finish(*item, idx % (ahead + 1), idx % p_sc.shape[0], pending.pop(0))


def _moba(q, k, vt, kmean, *, batch, seq, d):
    nb = seq // MOBA_BLOCK
    pairs = d // LANES
    head_dim = d // N_HEADS
    assert seq % MOBA_BLOCK == 0 and d % LANES == 0 and LANES // head_dim == 2
    slopes = LOG2E * jnp.exp2(-8.0 * jnp.arange(1, N_HEADS + 1, dtype=F32) / N_HEADS)
    grid_spec = pltpu.PrefetchScalarGridSpec(
        num_scalar_prefetch=1,
        grid=(batch * pairs,),
        in_specs=[
            pl.BlockSpec((1, seq, LANES), lambda bp, s: (bp // pairs, 0, bp % pairs)),
            pl.BlockSpec((1, seq, LANES), lambda bp, s: (bp // pairs, 0, bp % pairs)),
            pl.BlockSpec((1, LANES, seq), lambda bp, s: (bp // pairs, bp % pairs, 0)),
            pl.BlockSpec((1, nb, LANES), lambda bp, s: (bp // pairs, 0, bp % pairs)),
        ],
        out_specs=pl.BlockSpec((1, seq, LANES), lambda bp, s: (bp // pairs, 0, bp % pairs)),
        scratch_shapes=[
            pltpu.VMEM((LANES, seq), F32),
            pltpu.VMEM((MOBA_AHEAD + 1, seq, MOBA_BLOCK), F32),
            pltpu.VMEM((2, seq, MOBA_BLOCK), BF16),
        ],
    )
    o = pl.pallas_call(
        functools.partial(_moba_kernel, nb=nb, head_dim=head_dim, pairs=pairs),
        out_shape=jax.ShapeDtypeStruct((batch, seq, d), BF16),
        grid_spec=grid_spec,
        compiler_params=_params(("arbitrary",)),
        name="moba",
    )(slopes, q.reshape(batch, seq, d), k, vt, kmean)
    return o.reshape(batch * seq, d)


def kernel(x, ffn1_norm, ffn1_w_in, ffn1_w_out, mix_norm, ffn2_norm, ffn2_w_in, ffn2_w_out,
           s5_a_re, s5_a_im, s5_b_re, s5_b_im, s5_c_re, s5_c_im, s5_d, s5_log_step, s5_w_glu,
           kv_norm, w_kv, w_q, w_o, final_norm):
    batch, seq, d = x.shape
    depth = ffn1_norm.shape[0]
    n_a = s5_a_re.shape[0]
    h = x.reshape(batch * seq, d)
    s5_params = _s5_discretize(s5_a_re, s5_a_im, s5_b_re, s5_b_im, s5_c_re, s5_c_im, s5_log_step, batch)
    w_glu = s5_w_glu.astype(BF16)
    k = vt = kmean = None
    for layer in range(depth):
        if layer == n_a:
            k, vt, kmean = _kv_proj(h, kv_norm, w_kv, batch=batch)
        if layer < n_a:
            h = _ffn(h, ffn1_norm[layer], ffn1_w_in, ffn1_w_out, layer)
            h = _s5_mixer(h, mix_norm[layer], s5_params, s5_d[layer], w_glu, layer, batch=batch)
            attn = None
        else:
            j = layer - n_a
            h, q = _ffn(h, ffn1_norm[layer], ffn1_w_in, ffn1_w_out, layer,
                        q_proj=(mix_norm[layer], w_q[j], LOG2E * (d // N_HEADS) ** -0.5))
            attn = (_moba(q, k, vt, kmean, batch=batch, seq=seq, d=d), w_o[j])
        last = layer == depth - 1
        h = _ffn(h, ffn2_norm[layer], ffn2_w_in, ffn2_w_out, layer, attn=attn, final_g=final_norm if last else None)
    return h.reshape(batch, seq, d)
```

```python
import functools

import jax
import jax.numpy as jnp
from jax import lax
from jax.experimental import pallas as pl
from jax.experimental.pallas import tpu as pltpu

S5_GROUP = 16
S5_STATE = 64
N_HEADS = 16
MOBA_BLOCK = 256
MOBA_TOPK = 3
EPS = 1e-6
NEG = -1e30
LOG2E = 1.4426950408889634
MOBA_AHEAD = 3

LANES = 128
S5_CHUNK = LANES
S5_CHUNK_STATES = S5_CHUNK // S5_GROUP * S5_STATE
VMEM_LIMIT = 56 * 1024 * 1024

BF16 = jnp.bfloat16
F32 = jnp.float32
NT = (((1,), (1,)), ((), ()))


def _rms(x, g):
    return x * lax.rsqrt(jnp.mean(x * x, axis=-1, keepdims=True) + EPS) * g


def _const_spec(shape, index=None):
    index = (0,) * len(shape) if index is None else index
    return pl.BlockSpec(shape, lambda *_: index, pipeline_mode=pl.Buffered(1))


def _params(semantics):
    return pltpu.CompilerParams(dimension_semantics=semantics, vmem_limit_bytes=VMEM_LIMIT)


def _ffn_kernel(*refs, d_ff, ck, n_w, attn_in, q_out, final_norm, q_scale):
    refs = list(refs)
    x_ref = refs.pop(0)
    attn_ref, wo_ref = (refs.pop(0), refs.pop(0)) if attn_in else (None, None)
    g_ref, win_ref, wout_ref = refs.pop(0), refs.pop(0), refs.pop(0)
    gq_ref, wq_ref = (refs.pop(0), refs.pop(0)) if q_out else (None, None)
    fg_ref = refs.pop(0) if final_norm else None
    o_ref = refs.pop(0)
    q_ref = refs.pop(0) if q_out else None
    win_sc, wout_sc = refs
    step = pl.program_id(0)

    @pl.when(step < n_w)
    def _():
        r_in, r_out = win_ref.shape[0], wout_ref.shape[0]
        win_sc[pl.ds(pl.multiple_of(step * r_in, r_in), r_in), :] = win_ref[...].astype(BF16)
        wout_sc[pl.ds(pl.multiple_of(step * r_out, r_out), r_out), :] = wout_ref[...].astype(BF16)

    @pl.when(step >= n_w)
    def _():
        x = x_ref[...]
        if attn_in:
            x = x + jnp.dot(attn_ref[...], wo_ref[...], preferred_element_type=F32)
        xn = _rms(x, g_ref[...]).astype(BF16)
        acc = jnp.zeros(x.shape, F32)
        for c in range(d_ff // ck):
            gate = jnp.dot(xn, win_sc[:, c * ck:(c + 1) * ck], preferred_element_type=F32)
            up = jnp.dot(xn, win_sc[:, d_ff + c * ck:d_ff + (c + 1) * ck], preferred_element_type=F32)
            a = (gate * jax.nn.sigmoid(gate) * up).astype(BF16)
            acc = acc + jnp.dot(a, wout_sc[c * ck:(c + 1) * ck, :], preferred_element_type=F32)
        y = x + 0.5 * acc
        if q_out:
            yn = _rms(y, gq_ref[...]).astype(BF16)
            q_ref[...] = (jnp.dot(yn, wq_ref[...], preferred_element_type=F32) * q_scale).astype(BF16)
        if final_norm:
            y = _rms(y, fg_ref[...])
        o_ref[...] = y


def _ffn(h, g, w_in, w_out, layer, *, attn=None, q_proj=None, final_g=None, tm=1024, ck=256, n_w=16):
    t, d = h.shape
    d_ff = w_out.shape[1]
    bf16_rows = 16
    assert t % tm == 0 and d_ff % ck == 0 and d % (n_w * bf16_rows) == 0 and d_ff % (n_w * bf16_rows) == 0
    row_spec = pl.BlockSpec((tm, d), lambda i: (jnp.maximum(i - n_w, 0), 0))
    slab = lambda i: (layer, jnp.minimum(i, n_w - 1), 0)
    args, in_specs = [h], [row_spec]
    if attn is not None:
        args += [attn[0], attn[1].astype(BF16)]
        in_specs += [row_spec, _const_spec((d, d))]
    args += [g.reshape(1, d), w_in, w_out]
    in_specs += [_const_spec((1, d)), pl.BlockSpec((None, d // n_w, 2 * d_ff), slab),
                 pl.BlockSpec((None, d_ff // n_w, d), slab)]
    out_shape, out_specs = [jax.ShapeDtypeStruct((t, d), F32)], [row_spec]
    if q_proj is not None:
        args += [q_proj[0].reshape(1, d), q_proj[1].astype(BF16)]
        in_specs += [_const_spec((1, d)), _const_spec((d, d))]
        out_shape.append(jax.ShapeDtypeStruct((t, d), BF16))
        out_specs.append(row_spec)
    if final_g is not None:
        args.append(final_g.reshape(1, d))
        in_specs.append(_const_spec((1, d)))
    out = pl.pallas_call(
        functools.partial(_ffn_kernel, d_ff=d_ff, ck=ck, n_w=n_w, attn_in=attn is not None,
                          q_out=q_proj is not None, final_norm=final_g is not None,
                          q_scale=q_proj[2] if q_proj else None),
        out_shape=out_shape,
        grid=(n_w + t // tm,),
        in_specs=in_specs,
        out_specs=out_specs,
        scratch_shapes=[pltpu.VMEM((d, 2 * d_ff), BF16), pltpu.VMEM((d_ff, d), BF16)],
        compiler_params=_params(("arbitrary",)),
        name="ffn",
    )(*args)
    return out if q_proj is not None else out[0]


def _to_time_major(x, slab_sc, tl, batch):
    slabs = x.shape[1] // LANES
    for s in range(slabs):
        for b in range(batch):
            slab_sc[s, pl.ds(b, tl, stride=batch), :] = x[b * tl:(b + 1) * tl, s * LANES:(s + 1) * LANES]
    return jnp.concatenate([slab_sc[s] for s in range(slabs)], axis=-1)


def _to_batch_major(y, slab_sc, tl, batch):
    slabs = y.shape[1] // LANES
    for s in range(slabs):
        slab_sc[s] = y[:, s * LANES:(s + 1) * LANES]
    return jnp.concatenate(
        [jnp.concatenate([slab_sc[s, pl.ds(b, tl, stride=batch), :] for s in range(slabs)], axis=-1)
         for b in range(batch)], axis=0)


def _s5_pack_operators(bq_ref, cq_ref, cbq_ref, b2_ref, c2_ref, cb_ref, n_chunks):
    def replicate(x, reps):
        w = x.shape[1]
        sel = (lax.broadcasted_iota(jnp.int32, (w, w * reps), 1) % w
               == lax.broadcasted_iota(jnp.int32, (w, w * reps), 0)).astype(BF16)
        return jnp.dot(x.astype(BF16), sel, preferred_element_type=F32)

    def own_group(shape, rows_per_group, lanes_per_group):
        return (lax.broadcasted_iota(jnp.int32, shape, 0) // rows_per_group
                == lax.broadcasted_iota(jnp.int32, shape, 1) // lanes_per_group)

    gc = S5_CHUNK // S5_GROUP
    for j in range(n_chunks):
        keep = own_group((gc * S5_STATE, S5_CHUNK), S5_STATE, S5_GROUP)
        b2t = jnp.concatenate(
            [jnp.concatenate([jnp.where(keep, replicate(bq_ref[2 * r + q, j], gc), 0.0) for r in range(2)], axis=1)
             for q in range(2)], axis=0)
        b2_ref[j] = b2t.T.astype(BF16)
        keep = own_group((S5_CHUNK, gc * S5_STATE), S5_GROUP, S5_STATE)
        c2t = jnp.concatenate(
            [jnp.concatenate([jnp.where(keep, replicate(cq_ref[2 * s + q, j], gc), 0.0) for q in range(2)], axis=1)
             for s in range(2)], axis=0)
        c2_ref[j] = c2t.T.astype(BF16)
        keep = own_group((S5_CHUNK, S5_CHUNK), S5_GROUP, S5_GROUP)
        cb_ref[j] = jnp.where(keep, replicate(cbq_ref[j], gc), 0.0).astype(BF16)


def _s5_kernel(h_ref, hprev_ref, g_ref, lam_ref, bq_ref, cq_ref, cbq_ref, dskip_ref, wglu_ref, o_ref,
               state_sc, ynext_sc, z_sc, slab_in_sc, slab_out_sc, b2_ref, c2_ref, cb_ref, *,
               tl, n_chunks, batch):
    @pl.when(pl.program_id(0) == 0)
    def _():
        state_sc[...] = jnp.zeros(state_sc.shape, F32)
        ynext_sc[...] = jnp.zeros(ynext_sc.shape, F32)
        z_sc[...] = jnp.zeros(z_sc.shape, BF16)
        _s5_pack_operators(bq_ref, cq_ref, cbq_ref, b2_ref, c2_ref, cb_ref, n_chunks)

    ns = S5_CHUNK_STATES
    d = h_ref.shape[-1]
    half = tl // 2
    glu_cols = 2 * d // n_chunks
    z_prev = z_sc[...]
    hn = _to_time_major(_rms(h_ref[...].reshape(batch * tl, d), g_ref[...]), slab_in_sc, tl, batch)
    hn = hn.reshape(half, 2 * batch, d)
    u_e = hn[:, :batch].reshape(half * batch, d)
    u_o = hn[:, batch:].reshape(half * batch, d)
    ue_b, uo_b = u_e.astype(BF16), u_o.astype(BF16)
    def input_proj(j):
        cols = slice(j * S5_CHUNK, (j + 1) * S5_CHUNK)
        return jnp.dot(jnp.concatenate([uo_b[:, cols], ue_b[:, cols]], axis=-1), b2_ref[j],
                       preferred_element_type=F32)

    z_e, z_o, zz = [], [], []
    bu_next = input_proj(0)
    for j in range(n_chunks):
        cols = slice(j * S5_CHUNK, (j + 1) * S5_CHUNK)
        bu = bu_next
        if j + 1 < n_chunks:
            bu_next = input_proj(j + 1)
        zz.append(jnp.dot(z_prev, wglu_ref[:, j * glu_cols:(j + 1) * glu_cols], preferred_element_type=F32))
        lam = lam_ref[j]
        lr, li = lam[:, :ns], lam[:, ns:]
        x = state_sc[j]
        xs = []
        for m in range(half):
            bu_m = bu[m * batch:(m + 1) * batch]
            xr, xi = x[:, :ns], x[:, ns:]
            nr = lr * xr - li * xi + bu_m[:, :ns]
            ni = lr * xi + li * xr + bu_m[:, ns:]
            x = jnp.concatenate([nr, ni], axis=-1)
            xs.append(x)
        state_sc[j] = x
        y2 = jnp.dot(jnp.concatenate(xs, axis=0).astype(BF16), c2_ref[j], preferred_element_type=F32)
        y_o, y_next = y2[:, :S5_CHUNK], y2[:, S5_CHUNK:]
        y_e = jnp.concatenate([ynext_sc[j], y_next[:-batch]], axis=0)
        ynext_sc[j] = y_next[-batch:]
        y_e = y_e + jnp.dot(ue_b[:, cols], cb_ref[j], preferred_element_type=F32)
        z_e.append(jax.nn.gelu(y_e + dskip_ref[:, cols] * u_e[:, cols]).astype(BF16))
        z_o.append(jax.nn.gelu(y_o + dskip_ref[:, cols] * u_o[:, cols]).astype(BF16))
    z_sc[...] = jnp.concatenate([jnp.concatenate(z_e, axis=-1), jnp.concatenate(z_o, axis=-1)], axis=0)
    zz = jnp.concatenate(zz, axis=-1)
    mixed = zz[:, :d] * jax.nn.sigmoid(zz[:, d:])
    mixed = jnp.concatenate([mixed[:half * batch].reshape(half, batch, d),
                             mixed[half * batch:].reshape(half, batch, d)], axis=1).reshape(tl * batch, d)
    mixed = _to_batch_major(mixed, slab_out_sc, tl, batch)
    o_ref[...] = hprev_ref[...] + mixed.reshape(batch, tl, d)


def _s5_discretize(a_re, a_im, b_re, b_im, c_re, c_im, log_step, batch):
    n, g, p = a_re.shape
    gc = S5_CHUNK // S5_GROUP
    n_chunks = g // gc
    dt = jnp.exp(log_step)[..., None]
    ar = jnp.minimum(a_re, -1e-4)
    ai = a_im
    mag = jnp.exp(ar * dt)
    lr = mag * jnp.cos(ai * dt)
    li = mag * jnp.sin(ai * dt)
    nr = lr - 1.0
    den = ar * ar + ai * ai
    fr = (nr * ar + li * ai) / den
    fi = (li * ar - nr * ai) / den
    bbr = fr[..., None] * b_re - fi[..., None] * b_im
    bbi = fr[..., None] * b_im + fi[..., None] * b_re
    lbr = lr[..., None] * bbr - li[..., None] * bbi
    lbi = lr[..., None] * bbi + li[..., None] * bbr
    clr = c_re * lr[:, :, None, :] - c_im * li[:, :, None, :]
    cli = c_re * li[:, :, None, :] + c_im * lr[:, :, None, :]
    cb = jnp.einsum('ngjp,ngpk->ngkj', c_re, bbr) - jnp.einsum('ngjp,ngpk->ngkj', c_im, bbi)
    bq = jnp.stack([bbr, bbi, lbr, lbi]).reshape(4, n, n_chunks, gc * p, S5_GROUP)
    cq = jnp.stack([c_re, -c_im, clr, -cli]).reshape(4, n, n_chunks, S5_CHUNK, p)
    cbq = cb.reshape(n, n_chunks, S5_CHUNK, S5_GROUP)
    lam2 = jnp.concatenate([(lr * lr - li * li).reshape(n, n_chunks, 1, gc * p),
                            (2.0 * lr * li).reshape(n, n_chunks, 1, gc * p)], axis=-1)
    lam2 = jnp.broadcast_to(lam2, (n, n_chunks, batch, 2 * gc * p))
    return lam2, bq, cq, cbq


def _s5_mixer(h, g, s5_params, d_skip, w_glu, layer, *, batch, tl=64):
    t, d = h.shape
    seq = t // batch
    n_chunks = d // S5_CHUNK
    ns2 = 2 * S5_CHUNK_STATES
    rows = tl * batch
    nt = seq // tl
    assert batch == 8 and tl % 2 == 0 and seq % tl == 0 and d % S5_CHUNK == 0
    h3 = h.reshape(batch, seq, d)
    out = pl.pallas_call(
        functools.partial(_s5_kernel, tl=tl, n_chunks=n_chunks, batch=batch),
        out_shape=jax.ShapeDtypeStruct((batch, seq, d), F32),
        grid=(nt + 1,),
        in_specs=[
            pl.BlockSpec((batch, tl, d), lambda i: (0, jnp.minimum(i, nt - 1), 0)),
            pl.BlockSpec((batch, tl, d), lambda i: (0, jnp.maximum(i - 1, 0), 0)),
            _const_spec((1, d)),
            _const_spec((None, n_chunks, batch, ns2), (layer, 0, 0, 0)),
            _const_spec((4, None, n_chunks, S5_CHUNK_STATES, S5_GROUP), (0, layer, 0, 0, 0)),
            _const_spec((4, None, n_chunks, S5_CHUNK, S5_STATE), (0, layer, 0, 0, 0)),
            _const_spec((None, n_chunks, S5_CHUNK, S5_GROUP), (layer, 0, 0, 0)),
            _const_spec((1, d)),
            _const_spec((None, d, 2 * d), (layer, 0, 0)),
        ],
        out_specs=pl.BlockSpec((batch, tl, d), lambda i: (0, jnp.maximum(i - 1, 0), 0)),
        scratch_shapes=[
            pltpu.VMEM((n_chunks, batch, ns2), F32),
            pltpu.VMEM((n_chunks, batch, S5_CHUNK), F32),
            pltpu.VMEM((rows, d), BF16),
            pltpu.VMEM((d // LANES, rows, LANES), F32),
            pltpu.VMEM((d // LANES, rows, LANES), F32),
            pltpu.VMEM((n_chunks, 2 * S5_CHUNK, ns2), BF16),
            pltpu.VMEM((n_chunks, ns2, 2 * S5_CHUNK), BF16),
            pltpu.VMEM((n_chunks, S5_CHUNK, S5_CHUNK), BF16),
        ],
        compiler_params=_params(("arbitrary",)),
        name="s5_mixer",
    )(h3, h3, g.reshape(1, d), *s5_params, d_skip.reshape(1, d), w_glu)
    return out.reshape(t, d)


def _kv_kernel(x_ref, g_ref, wk_ref, wvt_ref, k_ref, vt_ref, km_ref, *, ck):
    seq, d = x_ref.shape[1:]
    rc = 2 * MOBA_BLOCK
    for r in range(seq // rc):
        rows = slice(r * rc, (r + 1) * rc)
        xn = _rms(x_ref[0, rows, :], g_ref[...]).astype(BF16)
        for c in range(d // ck):
            cols = slice(c * ck, (c + 1) * ck)
            kc = jnp.dot(xn, wk_ref[:, cols], preferred_element_type=F32)
            k_ref[0, rows, cols] = kc.astype(BF16)
            km_ref[0, r * (rc // MOBA_BLOCK):(r + 1) * (rc // MOBA_BLOCK), cols] = jnp.mean(
                kc.reshape(rc // MOBA_BLOCK, MOBA_BLOCK, ck), axis=1)
            vt_ref[0, cols, rows] = lax.dot_general(wvt_ref[cols, :], xn, NT,
                                                    preferred_element_type=F32).astype(BF16)


def _kv_proj(h, g, w_kv, *, batch, ck=256):
    t, d = h.shape
    seq = t // batch
    nb = seq // MOBA_BLOCK
    w_k = w_kv[:, :d].astype(BF16)
    w_vt = w_kv[:, d:].T.astype(BF16)
    return pl.pallas_call(
        functools.partial(_kv_kernel, ck=ck),
        out_shape=(jax.ShapeDtypeStruct((batch, seq, d), BF16),
                   jax.ShapeDtypeStruct((batch, d, seq), BF16),
                   jax.ShapeDtypeStruct((batch, nb, d), F32)),
        grid=(batch,),
        in_specs=[
            pl.BlockSpec((1, seq, d), lambda b: (b, 0, 0)),
            _const_spec((1, d)),
            _const_spec((d, d)),
            _const_spec((d, d)),
        ],
        out_specs=(pl.BlockSpec((1, seq, d), lambda b: (b, 0, 0)),
                   pl.BlockSpec((1, d, seq), lambda b: (b, 0, 0)),
                   pl.BlockSpec((1, nb, d), lambda b: (b, 0, 0))),
        compiler_params=_params(("arbitrary",)),
        name="kv_proj",
    )(h.reshape(batch, seq, d), g.reshape(1, d), w_k, w_vt)


def _moba_kernel(slopes_ref, q_ref, k_ref, vt_ref, km_ref, o_ref, ot_sc, t_sc, p_sc, *,
                 nb, head_dim, pairs):
    bs = MOBA_BLOCK
    pair = pl.program_id(0) % pairs
    heads = LANES // head_dim
    seq = q_ref.shape[1]
    q = q_ref[0].reshape(nb, bs, LANES)
    k = k_ref[0].reshape(nb, bs, LANES)
    km = km_ref[0].astype(BF16)
    lane = lax.broadcasted_iota(jnp.int32, (1, LANES), 1)
    pos = lax.broadcasted_iota(jnp.int32, (bs, LANES), 0).astype(F32)
    causal = (lax.broadcasted_iota(jnp.int32, (bs, bs), 1)
              >= lax.broadcasted_iota(jnp.int32, (bs, bs), 0))
    blk = lax.broadcasted_iota(jnp.int32, (nb, bs), 0)

    def split3(x):
        hi = x.astype(BF16).astype(F32)
        mid = (x - hi).astype(BF16).astype(F32)
        lo = (x - hi - mid).astype(BF16).astype(F32)
        return hi, mid, lo

    lane_b = lax.broadcasted_iota(jnp.int32, (bs, LANES), 1)

    def lanes_of(terms, base):
        out = jnp.zeros((bs, LANES), F32)
        for j, term in enumerate(terms):
            out = jnp.where(lane_b == base + j, term, out)
        return out.astype(BF16)

    slope, gate_t, q_aug, k_aug = [], [], [], []
    for hh in range(heads):
        slope.append(slopes_ref[pair * heads + hh])
        in_head = lane // head_dim == hh
        fb = ((hh + 1) % heads) * head_dim
        slope_v = jnp.full((bs, LANES), slope[hh], F32)
        q_feat = lanes_of(split3(slope_v) + split3(-slope_v * pos), fb)
        k_feat = lanes_of([pos] * 3 + [jnp.ones((bs, LANES), F32)] * 3, fb)
        q_aug.append(jnp.where(in_head, q, q_feat[None]).reshape(seq, LANES))
        k_aug.append(jnp.where(in_head, k, k_feat[None]).reshape(seq, LANES))
        gate_t.append(lax.dot_general(jnp.where(in_head, km, jnp.zeros_like(km)), q_aug[hh], NT,
                                      preferred_element_type=F32))

    def scores(i, hh, slot):
        s = lax.dot_general(k_aug[hh][0:(i + 1) * bs], q_aug[hh][i * bs:(i + 1) * bs], NT,
                            preferred_element_type=F32)
        col_max = []
        for n in range(i + 1):
            t_n = s[n * bs:(n + 1) * bs]
            if n == i:
                t_n = jnp.where(causal, t_n, NEG)
            t_sc[slot, n * bs:(n + 1) * bs, :] = t_n
            col_max.append(jnp.max(t_n, axis=0, keepdims=True))
        return col_max

    def finish(i, hh, slot, p_slot, col_max):
        if i > MOBA_TOPK:
            g = gate_t[hh][:, i * bs:(i + 1) * bs]
            rank = jnp.zeros((nb, bs), jnp.int32)
            for m in range(i):
                gm = g[m:m + 1, :]
                rank = rank + ((gm > g) | ((gm == g) & (m < blk))).astype(jnp.int32)
            sel = rank < MOBA_TOPK
        bias = []
        for n in range(i):
            b_n = -slope[hh] * float((i - n) * bs)
            if i > MOBA_TOPK:
                b_n = jnp.where(sel[n:n + 1, :], b_n, NEG)
            bias.append(b_n)
        bias.append(0.0)
        m_row = col_max[i]
        for n in range(i):
            m_row = jnp.maximum(m_row, col_max[n] + bias[n])
        l_row = jnp.zeros((1, bs), F32)
        for n in range(i + 1):
            p_n = jnp.exp2(t_sc[slot, n * bs:(n + 1) * bs, :] - (m_row - bias[n]))
            l_row = l_row + jnp.sum(p_n, axis=0, keepdims=True)
            p_sc[p_slot, n * bs:(n + 1) * bs, :] = p_n.astype(BF16)
        o_t = jnp.dot(vt_ref[0, hh * head_dim:(hh + 1) * head_dim, 0:(i + 1) * bs],
                      p_sc[p_slot, 0:(i + 1) * bs, :], preferred_element_type=F32)
        ot_sc[hh * head_dim:(hh + 1) * head_dim, i * bs:(i + 1) * bs] = o_t / l_row
        if hh == heads - 1:
            o_ref[0, i * bs:(i + 1) * bs, :] = ot_sc[:, i * bs:(i + 1) * bs].T.astype(BF16)

    items = [(i, hh) for i in reversed(range(nb)) for hh in range(heads)]
    ahead = t_sc.shape[0] - 1
    pending = [scores(*item, slot) for slot, item in enumerate(items[:ahead])]
    for idx, item in enumerate(items):
        if idx + ahead < len(items):
            pending.append(scores(*items[idx + ahead], (idx + ahead) % (ahead + 1)))
        finish(*item, idx % (ahead + 1), idx % p_sc.shape[0], pending.pop(0))


def _moba(q, k, vt, kmean, *, batch, seq, d):
    nb = seq // MOBA_BLOCK
    pairs = d // LANES
    head_dim = d // N_HEADS
    assert seq % MOBA_BLOCK == 0 and d % LANES == 0 and LANES // head_dim == 2
    slopes = LOG2E * jnp.exp2(-8.0 * jnp.arange(1, N_HEADS + 1, dtype=F32) / N_HEADS)
    grid_spec = pltpu.PrefetchScalarGridSpec(
        num_scalar_prefetch=1,
        grid=(batch * pairs,),
        in_specs=[
            pl.BlockSpec((1, seq, LANES), lambda bp, s: (bp // pairs, 0, bp % pairs)),
            pl.BlockSpec((1, seq, LANES), lambda bp, s: (bp // pairs, 0, bp % pairs)),
            pl.BlockSpec((1, LANES, seq), lambda bp, s: (bp // pairs, bp % pairs, 0)),
            pl.BlockSpec((1, nb, LANES), lambda bp, s: (bp // pairs, 0, bp % pairs)),
        ],
        out_specs=pl.BlockSpec((1, seq, LANES), lambda bp, s: (bp // pairs, 0, bp % pairs)),
        scratch_shapes=[
            pltpu.VMEM((LANES, seq), F32),
            pltpu.VMEM((MOBA_AHEAD + 1, seq, MOBA_BLOCK), F32),
            pltpu.VMEM((2, seq, MOBA_BLOCK), BF16),
        ],
    )
    o = pl.pallas_call(
        functools.partial(_moba_kernel, nb=nb, head_dim=head_dim, pairs=pairs),
        out_shape=jax.ShapeDtypeStruct((batch, seq, d), BF16),
        grid_spec=grid_spec,
        compiler_params=_params(("arbitrary",)),
        name="moba",
    )(slopes, q.reshape(batch, seq, d), k, vt, kmean)
    return o.reshape(batch * seq, d)


def kernel(x, ffn1_norm, ffn1_w_in, ffn1_w_out, mix_norm, ffn2_norm, ffn2_w_in, ffn2_w_out,
           s5_a_re, s5_a_im, s5_b_re, s5_b_im, s5_c_re, s5_c_im, s5_d, s5_log_step, s5_w_glu,
           kv_norm, w_kv, w_q, w_o, final_norm):
    batch, seq, d = x.shape
    depth = ffn1_norm.shape[0]
    n_a = s5_a_re.shape[0]
    h = x.reshape(batch * seq, d)
    s5_params = _s5_discretize(s5_a_re, s5_a_im, s5_b_re, s5_b_im, s5_c_re, s5_c_im, s5_log_step, batch)
    w_glu = s5_w_glu.astype(BF16)
    k = vt = kmean = None
    for layer in range(depth):
        if layer == n_a:
            k, vt, kmean = _kv_proj(h, kv_norm, w_kv, batch=batch)
        if layer < n_a:
            h = _ffn(h, ffn1_norm[layer], ffn1_w_in, ffn1_w_out, layer)
            h = _s5_mixer(h, mix_norm[layer], s5_params, s5_d[layer], w_glu, layer, batch=batch)
            attn = None
        else:
            j = layer - n_a
            h, q = _ffn(h, ffn1_norm[layer], ffn1_w_in, ffn1_w_out, layer,
                        q_proj=(mix_norm[layer], w_q[j], LOG2E * (d // N_HEADS) ** -0.5))
            attn = (_moba(q, k, vt, kmean, batch=batch, seq=seq, d=d), w_o[j])
        last = layer == depth - 1
        h = _ffn(h, ffn2_norm[layer], ffn2_w_in, ffn2_w_out, layer, attn=attn, final_g=final_norm if last else None)
    return h.reshape(batch, seq, d)
```
